```python
import math
import jax, jax.numpy as jnp
from jax import lax
import numpy as np

D_MODEL = 1024
BATCH = 2
SEQ = 8192
DEPTH = 2
DEC_BATCH = 32
DEC_SEQ = 4
PAST_LEN = 8192
PAGE_SIZE = 128

H_A = 8
DH_A = 64
H_I = 8
D_I = 64
TOPK = 256
Q_BLOCK = 128
H_B = 4
DK_B = 64
DV_B = 64
H_C = 4
P_C = 64
G_C = 2
N_C = 128
CONV_W = 4
CHUNK = 64
W_A = H_A * DH_A
W_B = H_B * DV_B
W_C = H_C * P_C
D_MIX = W_A + W_B + W_C
CB = 2 * H_B * DK_B + H_B * DV_B
CC = W_C + 2 * G_C * N_C
D_IN = 3 * W_A + H_I * D_I + D_I + H_I + CB + W_B + 2 * H_B + W_C + CC + H_C
D_FF = 3584
N_EXPERTS = 8
TOP_E = 2
D_FF_E = 3584
D_PLE = 256
EPS = 1e-6

kernel_name = 'hymba_dsa_gdn_ssd_decoder_step'

F32 = jnp.float32


def split_points():
    sizes = (W_A, W_A, W_A, H_I * D_I, D_I, H_I, CB, W_B, H_B, H_B, W_C, CC, H_C)
    return tuple(int(s) for s in np.cumsum(sizes)[:-1])


def rmsnorm(x, g):
    xf = x.astype(F32)
    y = xf * lax.rsqrt(jnp.mean(xf * xf, axis=-1, keepdims=True) + EPS)
    return (y * g.astype(F32)).astype(x.dtype)


def l2norm(x):
    xf = x.astype(F32)
    return xf * lax.rsqrt(jnp.sum(xf * xf, axis=-1, keepdims=True) + EPS)


def causal_conv(x, buf, w, b=None):
    T = x.shape[1]
    xp = jnp.concatenate([buf.astype(x.dtype), x], axis=1)
    y = xp[:, 0:T] * w[0]
    for i in range(1, CONV_W):
        y = y + xp[:, i:i + T] * w[i]
    if b is not None:
        y = y + b
    return y, xp[:, T:]


def to_chunks(a, C):
    B, T, H = a.shape[:3]
    a = a.reshape((B, T // C, C, H) + a.shape[3:])
    return jnp.moveaxis(a, 3, 1)


def from_chunks(o):
    n, B, H, C, D = o.shape
    return o.transpose(1, 0, 3, 2, 4).reshape(B, n * C, H, D)


def gated_delta_chunked(q, k, v, g, beta, S0):
    B, T, H, DK = q.shape
    C = CHUNK if T % CHUNK == 0 else T
    q = to_chunks(q.astype(F32) * DK ** -0.5, C)
    k = to_chunks(k.astype(F32), C)
    v = to_chunks(v.astype(F32), C)
    g = to_chunks(g.astype(F32), C)
    beta = to_chunks(beta.astype(F32), C)
    G = jnp.cumsum(g, axis=-1)
    i = jnp.arange(C)
    causal = i[:, None] >= i[None, :]
    decay = jnp.exp(jnp.where(causal, G[..., :, None] - G[..., None, :], -jnp.inf))
    kb = k * beta[..., None]
    A = jnp.where(i[:, None] > i[None, :], jnp.einsum('bhnid,bhnjd->bhnij', kb, k) * decay, 0.0)
    eye = jnp.eye(C, dtype=F32)
    Tinv = lax.linalg.triangular_solve(A + eye, jnp.broadcast_to(eye, A.shape),
                                       left_side=True, lower=True, unit_diagonal=True)
    u0 = jnp.einsum('bhnij,bhnjv->bhniv', Tinv, v * beta[..., None])
    w = jnp.einsum('bhnij,bhnjd->bhnid', Tinv, kb * jnp.exp(G)[..., None])
    attn = jnp.einsum('bhnid,bhnjd->bhnij', q, k) * decay
    qg = q * jnp.exp(G)[..., None]
    kg = k * jnp.exp(G[..., -1:] - G)[..., None]
    gl = jnp.exp(G[..., -1])

    def step(S, xs):
        u0c, wc, attnc, qgc, kgc, glc = xs
        u = u0c - jnp.einsum('bhid,bhdv->bhiv', wc, S)
        o = jnp.einsum('bhid,bhdv->bhiv', qgc, S) + jnp.einsum('bhij,bhjv->bhiv', attnc, u)
        S = S * glc[..., None, None] + jnp.einsum('bhid,bhiv->bhdv', kgc, u)
        return S, o

    xs = tuple(jnp.moveaxis(a, 2, 0) for a in (u0, w, attn, qg, kg, gl))
    S, o = lax.scan(step, S0.astype(F32), xs)
    return from_chunks(o), S


def ssd_chunked(x, dt, A, Bm, Cm, h0):
    B, T, H, P = x.shape
    C = CHUNK if T % CHUNK == 0 else T
    rep = H // Bm.shape[2]
    Bm = to_chunks(jnp.repeat(Bm.astype(F32), rep, axis=2), C)
    Cm = to_chunks(jnp.repeat(Cm.astype(F32), rep, axis=2), C)
    dt = dt.astype(F32)
    xdt = to_chunks(x.astype(F32) * dt[..., None], C)
    Acum = jnp.cumsum(to_chunks(dt * A.astype(F32), C), axis=-1)
    i = jnp.arange(C)
    L = jnp.exp(jnp.where(i[:, None] >= i[None, :], Acum[..., :, None] - Acum[..., None, :], -jnp.inf))
    y_diag = jnp.einsum('bhcij,bhcjp->bhcip', jnp.einsum('bhcin,bhcjn->bhcij', Cm, Bm) * L, xdt)
    states = jnp.einsum('bhcjn,bhcjp->bhcpn', Bm * jnp.exp(Acum[..., -1:] - Acum)[..., None], xdt)

    def step(hprev, xs):
        st, dc = xs
        return hprev * dc[..., None, None] + st, hprev

    hT, hprev = lax.scan(step, h0.astype(F32),
                         (jnp.moveaxis(states, 2, 0), jnp.moveaxis(jnp.exp(Acum[..., -1]), 2, 0)))
    y_off = jnp.einsum('bhcin,bhcpn->bhcip', Cm * jnp.exp(Acum)[..., None], jnp.moveaxis(hprev, 0, 2))
    return from_chunks(jnp.moveaxis(y_diag + y_off, 2, 0)), hT


def take_rows(a, idx):
    return jax.vmap(lambda ab, ib: ab[ib])(a, idx)


def index_scores(qi, wi, ki):
    s = jax.nn.relu(jnp.einsum('bqhd,bsd->bqhs', qi, ki).astype(F32) * D_I ** -0.5)
    return jnp.einsum('bqhs,bqh->bqs', s, wi.astype(F32) * H_I ** -0.5)


def sparse_attend(q, ks, vs, valid):
    logits = jnp.einsum('bqhd,bqkhd->bqhk', q, ks).astype(F32) * DH_A ** -0.5
    logits = jnp.where(valid[:, :, None, :], logits, -jnp.inf)
    p = jax.nn.softmax(logits, axis=-1)
    return jnp.einsum('bqhk,bqkhd->bqhd', p.astype(vs.dtype), vs)


def dsa_prompt(q, k, v, qi, ki, wi):
    B, T = q.shape[:2]
    K = min(TOPK, T // 4)
    key_pos = jnp.arange(T)

    def block(bi):
        t0 = bi * Q_BLOCK
        qb = lax.dynamic_slice_in_dim(q, t0, Q_BLOCK, axis=1)
        qib = lax.dynamic_slice_in_dim(qi, t0, Q_BLOCK, axis=1)
        wib = lax.dynamic_slice_in_dim(wi, t0, Q_BLOCK, axis=1)
        pos = t0 + jnp.arange(Q_BLOCK)
        scores = jnp.where(key_pos[None, None, :] <= pos[None, :, None],
                           index_scores(qib, wib, ki), -jnp.inf)
        _, idx = lax.top_k(scores, K)
        valid = idx <= pos[None, :, None]
        return sparse_attend(qb, take_rows(k, idx), take_rows(v, idx), valid)

    out = lax.map(block, jnp.arange(T // Q_BLOCK))
    return jnp.moveaxis(out, 0, 1).reshape(B, T, W_A)


def dsa_sample(q, k, v, qi, ki, wi, cache_k, cache_v, cache_ki, page_table):
    B, T = q.shape[:2]
    past = page_table.shape[1] * PAGE_SIZE
    L = past + T
    K = min(TOPK, L // 4)
    ki_all = jnp.concatenate([cache_ki[page_table].reshape(B, past, D_I).astype(ki.dtype), ki], axis=1)
    pos = past + jnp.arange(T)
    scores = jnp.where(jnp.arange(L)[None, None, :] <= pos[None, :, None],
                       index_scores(qi, wi, ki_all), -jnp.inf)
    _, idx = lax.top_k(scores, K)
    valid = idx <= pos[None, :, None]
    from_past = (idx < past)[..., None, None]
    idx_p = jnp.minimum(idx, past - 1)
    phys = jax.vmap(lambda pt, i: pt[i])(page_table, idx_p // PAGE_SIZE)
    off = idx_p % PAGE_SIZE
    idx_n = jnp.clip(idx - past, 0, T - 1)
    ks = jnp.where(from_past, cache_k[phys, off].astype(k.dtype), take_rows(k, idx_n))
    vs = jnp.where(from_past, cache_v[phys, off].astype(v.dtype), take_rows(v, idx_n))
    return sparse_attend(q, ks, vs, valid).reshape(B, T, W_A)


def gdn_mixer(qkv, z, beta_raw, a_raw, buf, S0, conv_w, A_log, dt_bias, norm_g):
    B, T, _ = qkv.shape
    qkv, new_buf = causal_conv(qkv, buf, conv_w)
    qkv = jax.nn.silu(qkv)
    q, k, v = jnp.split(qkv, [H_B * DK_B, 2 * H_B * DK_B], axis=-1)
    q = l2norm(q.reshape(B, T, H_B, DK_B))
    k = l2norm(k.reshape(B, T, H_B, DK_B))
    v = v.reshape(B, T, H_B, DV_B)
    beta = jax.nn.sigmoid(beta_raw.astype(F32))
    g = -jnp.exp(A_log.astype(F32)) * jax.nn.softplus(a_raw.astype(F32) + dt_bias.astype(F32))
    o, S = gated_delta_chunked(q, k, v, g, beta, S0)
    o = rmsnorm(o, norm_g) * jax.nn.silu(z.reshape(B, T, H_B, DV_B).astype(F32))
    return o.reshape(B, T, W_B).astype(z.dtype), new_buf, S.astype(S0.dtype)


def mamba2_mixer(zc, xbc, dt_raw, buf, h0, conv_w, conv_b, A_log, dt_bias, Dskip, norm_g):
    B, T, _ = xbc.shape
    xbc, new_buf = causal_conv(xbc, buf, conv_w, conv_b)
    xbc = jax.nn.silu(xbc)
    xs, Bm, Cm = jnp.split(xbc, [W_C, W_C + G_C * N_C], axis=-1)
    xs = xs.reshape(B, T, H_C, P_C)
    dt = jax.nn.softplus(dt_raw.astype(F32) + dt_bias.astype(F32))
    y, hT = ssd_chunked(xs, dt, -jnp.exp(A_log.astype(F32)),
                        Bm.reshape(B, T, G_C, N_C), Cm.reshape(B, T, G_C, N_C), h0)
    y = y + Dskip.astype(F32)[:, None] * xs.astype(F32)
    yg = (y.reshape(B, T, W_C) * jax.nn.silu(zc.astype(F32))).reshape(B, T, G_C, W_C // G_C)
    yg = yg * lax.rsqrt(jnp.mean(yg * yg, axis=-1, keepdims=True) + EPS)
    y = yg.reshape(B, T, W_C) * norm_g.astype(F32)
    return y.astype(zc.dtype), new_buf, hT.astype(h0.dtype)


def swiglu(u, w1, w3, w2):
    return (jax.nn.silu(u @ w1) * (u @ w3)) @ w2


def moe_swiglu(u, w_router, w1, w3, w2):
    logits = (u @ w_router).astype(F32)
    vals, idx = lax.top_k(logits, TOP_E)
    gates = jax.nn.softmax(vals, axis=-1)
    comb = jnp.sum(jax.nn.one_hot(idx, N_EXPERTS, dtype=F32) * gates[..., None], axis=-2).astype(u.dtype)
    out = comb[..., 0:1] * swiglu(u, w1[0], w3[0], w2[0])
    for e in range(1, N_EXPERTS):
        out = out + comb[..., e:e + 1] * swiglu(u, w1[e], w3[e], w2[e])
    return out


def trunk(x, p, W, conv_b0, ssm_b0, conv_c0, ssm_c0, kv=None):
    B, T, _ = x.shape
    h = x
    rec = ([], [], [], [], [], [], [])
    for l in range(DEPTH):
        u = rmsnorm(h, W['g_mix'][l])
        (aq, ak, av, aiq, aik, aiw, bqkv, bz, bbeta, ba, cz, cxbc, cdt) = jnp.split(
            u @ W['w_in'][l], split_points(), axis=-1)
        q = rmsnorm(aq.reshape(B, T, H_A, DH_A), W['q_norm'][l])
        k = rmsnorm(ak.reshape(B, T, H_A, DH_A), W['k_norm'][l])
        v = av.reshape(B, T, H_A, DH_A)
        qi = aiq.reshape(B, T, H_I, D_I)
        if kv is None:
            ya = dsa_prompt(q, k, v, qi, aik, aiw)
        else:
            ya = dsa_sample(q, k, v, qi, aik, aiw, kv[0][l], kv[1][l], kv[2][l], kv[3])
        yb, bconv, bS = gdn_mixer(bqkv, bz, bbeta, ba, conv_b0[l], ssm_b0[l], W['b_conv_w'][l],
                                  W['b_A_log'][l], W['b_dt_bias'][l], W['b_norm'][l])
        yc, cconv, cS = mamba2_mixer(cz, cxbc, cdt, conv_c0[l], ssm_c0[l], W['c_conv_w'][l], W['c_conv_b'][l],
                                     W['c_A_log'][l], W['c_dt_bias'][l], W['c_D'][l], W['c_norm'][l])
        h = h + jnp.concatenate([ya, yb, yc], axis=-1) @ W['w_out'][l]
        u2 = rmsnorm(h, W['g_ffn'][l])
        if l % 2 == 0:
            j = l // 2
            h = h + swiglu(u2, W['ffn_w1'][j], W['ffn_w3'][j], W['ffn_w2'][j])
        else:
            j = l // 2
            h = h + moe_swiglu(u2, W['moe_router'][j], W['moe_w1'][j], W['moe_w3'][j], W['moe_w2'][j])
        gate = jax.nn.sigmoid((rmsnorm(h, W['g_ple'][l]) @ W['w_ple_gate'][l]).astype(F32)).astype(h.dtype)
        h = h + gate * (p[l] @ W['w_ple_proj'][l])
        for r, a in zip(rec, (k, v, aik, bconv, bS, cconv, cS)):
            r.append(a)
    return h, [jnp.stack(r) for r in rec]


def setup_inputs(seed: int = 0) -> dict:
    key = jax.random.key(seed)
    keys = list(jax.random.split(key, 48))

    def nrm(shape, scale=1.0):
        return jax.random.normal(keys.pop(), shape, jnp.float32) * scale

    def gain(shape):
        return 1.0 + nrm(shape, 0.02)

    def dt_bias(shape):
        dt = jnp.exp(jax.random.uniform(keys.pop(), shape, jnp.float32, math.log(1e-3), math.log(1e-1)))
        return dt + jnp.log(-jnp.expm1(-dt))

    def a_log(shape):
        return jnp.log(jax.random.uniform(keys.pop(), shape, jnp.float32, 1.0, 16.0))

    n_pages = PAST_LEN // PAGE_SIZE
    n_pool = (DEC_BATCH * n_pages * 5) // 4
    n_dense = (DEPTH + 1) // 2
    n_moe = DEPTH // 2
    perm = jax.random.permutation(keys.pop(), n_pool)[:DEC_BATCH * n_pages]
    return {
        'x_prompt': nrm((BATCH, SEQ, D_MODEL)),
        'x_sample': nrm((DEC_BATCH, DEC_SEQ, D_MODEL)),
        'cache_k': nrm((DEPTH, n_pool, PAGE_SIZE, H_A, DH_A)),
        'cache_v': nrm((DEPTH, n_pool, PAGE_SIZE, H_A, DH_A)),
        'cache_kidx': nrm((DEPTH, n_pool, PAGE_SIZE, D_I)),
        'state_b_conv': nrm((DEPTH, DEC_BATCH, CONV_W - 1, CB)),
        'state_b_ssm': nrm((DEPTH, DEC_BATCH, H_B, DK_B, DV_B), 0.1),
        'state_c_conv': nrm((DEPTH, DEC_BATCH, CONV_W - 1, CC)),
        'state_c_ssm': nrm((DEPTH, DEC_BATCH, H_C, P_C, N_C), 0.1),
        'page_table': perm.reshape(DEC_BATCH, n_pages).astype(jnp.int32),
        'p_prompt': nrm((DEPTH, BATCH, SEQ, D_PLE)),
        'p_sample': nrm((DEPTH, DEC_BATCH, DEC_SEQ, D_PLE)),
        'w_in': nrm((DEPTH, D_MODEL, D_IN), D_MODEL ** -0.5),
        'w_out': nrm((DEPTH, D_MIX, D_MODEL), D_MIX ** -0.5),
        'g_mix': gain((DEPTH, D_MODEL)),
        'g_ffn': gain((DEPTH, D_MODEL)),
        'g_ple': gain((DEPTH, D_MODEL)),
        'q_norm': gain((DEPTH, DH_A)),
        'k_norm': gain((DEPTH, DH_A)),
        'b_conv_w': nrm((DEPTH, CONV_W, CB), 0.5),
        'b_A_log': a_log((DEPTH, H_B)),
        'b_dt_bias': dt_bias((DEPTH, H_B)),
        'b_norm': gain((DEPTH, DV_B)),
        'c_conv_w': nrm((DEPTH, CONV_W, CC), 0.5),
        'c_conv_b': nrm((DEPTH, CC), 0.02),
        'c_A_log': a_log((DEPTH, H_C)),
        'c_dt_bias': dt_bias((DEPTH, H_C)),
        'c_D': gain((DEPTH, H_C)),
        'c_norm': gain((DEPTH, W_C)),
        'ffn_w1': nrm((n_dense, D_MODEL, D_FF), D_MODEL ** -0.5),
        'ffn_w3': nrm((n_dense, D_MODEL, D_FF), D_MODEL ** -0.5),
        'ffn_w2': nrm((n_dense, D_FF, D_MODEL), D_FF ** -0.5),
        'moe_router': nrm((n_moe, D_MODEL, N_EXPERTS), D_MODEL ** -0.5),
        'moe_w1': nrm((n_moe, N_EXPERTS, D_MODEL, D_FF_E), D_MODEL ** -0.5),
        'moe_w3': nrm((n_moe, N_EXPERTS, D_MODEL, D_FF_E), D_MODEL ** -0.5),
        'moe_w2': nrm((n_moe, N_EXPERTS, D_FF_E, D_MODEL), D_FF_E ** -0.5),
        'w_ple_gate': nrm((DEPTH, D_MODEL, D_MODEL), D_MODEL ** -0.5),
        'w_ple_proj': nrm((DEPTH, D_PLE, D_MODEL), D_PLE ** -0.5),
    }


def reference(x_prompt, x_sample, cache_k, cache_v, cache_kidx, state_b_conv, state_b_ssm, state_c_conv,
              state_c_ssm, page_table, p_prompt, p_sample, w_in, w_out, g_mix, g_ffn, g_ple, q_norm, k_norm,
              b_conv_w, b_A_log, b_dt_bias, b_norm, c_conv_w, c_conv_b, c_A_log, c_dt_bias, c_D, c_norm,
              ffn_w1, ffn_w3, ffn_w2, moe_router, moe_w1, moe_w3, moe_w2, w_ple_gate, w_ple_proj):
    W = dict(w_in=w_in, w_out=w_out, g_mix=g_mix, g_ffn=g_ffn, g_ple=g_ple, q_norm=q_norm, k_norm=k_norm,
             b_conv_w=b_conv_w, b_A_log=b_A_log, b_dt_bias=b_dt_bias, b_norm=b_norm,
             c_conv_w=c_conv_w, c_conv_b=c_conv_b, c_A_log=c_A_log, c_dt_bias=c_dt_bias, c_D=c_D, c_norm=c_norm,
             ffn_w1=ffn_w1, ffn_w3=ffn_w3, ffn_w2=ffn_w2, moe_router=moe_router, moe_w1=moe_w1,
             moe_w3=moe_w3, moe_w2=moe_w2, w_ple_gate=w_ple_gate, w_ple_proj=w_ple_proj)
    bp = x_prompt.shape[0]
    dtp = x_prompt.dtype
    y_p, (k_p, v_p, ki_p, bc_p, bs_p, cc_p, cs_p) = trunk(
        x_prompt, p_prompt, W,
        jnp.zeros((DEPTH, bp, CONV_W - 1, CB), dtp), jnp.zeros((DEPTH, bp, H_B, DK_B, DV_B), dtp),
        jnp.zeros((DEPTH, bp, CONV_W - 1, CC), dtp), jnp.zeros((DEPTH, bp, H_C, P_C, N_C), dtp))
    y_s, (k_s, v_s, ki_s, bc_s, bs_s, cc_s, cs_s) = trunk(
        x_sample, p_sample, W, state_b_conv, state_b_ssm, state_c_conv, state_c_ssm,
        (cache_k, cache_v, cache_kidx, page_table))
    return (y_p, y_s, k_p, k_s, v_p, v_s, ki_p, ki_s, bc_p, bc_s, bs_p, bs_s, cc_p, cc_s, cs_p, cs_s)
```

```python
import functools
import math

import numpy as np
import jax
import jax.numpy as jnp
from jax import lax
from jax.experimental import pallas as pl
from jax.experimental.pallas import tpu as pltpu

F32 = jnp.float32
I32 = jnp.int32
MM = jnp.bfloat16
EPS = 1e-6
NEG = -1e30
INT_MIN = -2 ** 31

LANES = 128
SUBLANES = 8
VMEM_LIMIT = 56 * 1024 * 1024

H_A, DH_A, H_I, D_I, TOPK, Q_BLOCK = 8, 64, 8, 64, 256, 128
H_B, DK_B, DV_B = 4, 64, 64
H_C, P_C, G_C, N_C = 4, 64, 2, 128
CONV_W, CHUNK = 4, 64
N_EXPERTS, TOP_E = 8, 2
W_A, W_B, W_C = H_A * DH_A, H_B * DV_B, H_C * P_C
CB = 2 * H_B * DK_B + H_B * DV_B
CC = W_C + 2 * G_C * N_C

SM_KI, SM_WI, SM_BETA, SM_A, SM_DT = 0, 64, 72, 76, 80


def _dot(a, b):
    return jnp.dot(a, b, preferred_element_type=F32)


def _dot_nt(a, b):
    return lax.dot_general(a, b, (((1,), (1,)), ((), ())), preferred_element_type=F32)


def _dot_tn(a, b):
    return lax.dot_general(a, b, (((0,), (0,)), ((), ())), preferred_element_type=F32)


def _split2(x):
    hi = x.astype(MM)
    lo = (x - hi.astype(F32)).astype(MM)
    return hi, lo


def _split3(x):
    a1 = x.astype(MM)
    r = x - a1.astype(F32)
    a2 = r.astype(MM)
    a3 = (r - a2.astype(F32)).astype(MM)
    return a1, a2, a3


def _dot_hi(a, b, f=_dot):
    ah, al = _split2(a)
    bh, bl = _split2(b)
    return f(ah, bh) + (f(ah, bl) + f(al, bh))


def _dot_x3(a, b_exact):
    a1, a2, a3 = _split3(a)
    return _dot(a1, b_exact) + (_dot(a2, b_exact) + _dot(a3, b_exact))


def _x3_dot(a_exact, b):
    b1, b2, b3 = _split3(b)
    return _dot(a_exact, b1) + (_dot(a_exact, b2) + _dot(a_exact, b3))


def _group_mean(xx, bd):
    hi, lo = _split2(xx)
    return _dot(hi, bd) + _dot(lo, bd)


def _sigmoid(x):
    return 1.0 / (1.0 + jnp.exp(-x))


def _silu(x):
    return x * _sigmoid(x)


def _softplus(x):
    return jnp.maximum(x, 0.0) + jnp.log(1.0 + jnp.exp(-jnp.abs(x)))


def _rms(x, g):
    ms = jnp.mean(x * x, axis=-1, keepdims=True)
    return x * lax.rsqrt(ms + EPS) * g


def _block_diag(width, group, value):
    i = np.arange(width)
    return jnp.asarray(np.where((i[:, None] // group) == (i[None, :] // group), value, 0.0), MM)


def _params(sem):
    return pltpu.CompilerParams(dimension_semantics=sem, vmem_limit_bytes=VMEM_LIMIT)


def _full(shape):
    n = len(shape)
    return pl.BlockSpec(shape, lambda *_: (0,) * n)


def _inproj_kernel(h_ref, g_ref, w_ref, qn_ref, kn_ref, bd_ref,
                   q16_ref, k32_ref, k16_ref, v32_ref, v16_ref, qi16_ref,
                   sm32_ref, sm16_ref, bqkv_ref, bz_ref, cz_ref, cxbc_ref):
    u = _rms(h_ref[...], g_ref[...]).astype(MM)

    def proj(a, b):
        return _dot(u, w_ref[:, a:b])

    bd = bd_ref[...]
    aq = proj(0, W_A)
    q = aq * lax.rsqrt(_group_mean(aq * aq, bd) + EPS) * qn_ref[...]
    q16_ref[...] = (q * (DH_A ** -0.5)).astype(MM)
    ak = proj(W_A, 2 * W_A)
    k = ak * lax.rsqrt(_group_mean(ak * ak, bd) + EPS) * kn_ref[...]
    k32_ref[...] = k
    k16_ref[...] = k.astype(MM)
    v = proj(2 * W_A, 3 * W_A)
    v32_ref[...] = v
    v16_ref[...] = v.astype(MM)
    qi16_ref[...] = proj(3 * W_A, 4 * W_A).astype(MM)
    o = 4 * W_A
    sm = proj(o, o + LANES)
    sm32_ref[...] = sm
    sm16_ref[...] = sm.astype(MM)
    o += LANES
    bqkv_ref[...] = proj(o, o + CB)
    bz_ref[...] = proj(o + CB, o + CB + W_B)
    o += CB + W_B
    cz_ref[...] = proj(o, o + W_C)
    cxbc_ref[...] = proj(o + W_C, o + W_C + CC)


def _arrange_w_in(w):
    d = w.shape[0]
    o_aik = 4 * W_A
    o_aiw = o_aik + D_I
    o_bqkv = o_aiw + H_I
    o_bz = o_bqkv + CB
    o_bbeta = o_bz + W_B
    o_ba = o_bbeta + H_B
    o_cz = o_ba + H_B
    o_cxbc = o_cz + W_C
    o_cdt = o_cxbc + CC
    small = jnp.concatenate(
        [w[:, o_aik:o_aiw], w[:, o_aiw:o_bqkv], w[:, o_bbeta:o_ba], w[:, o_ba:o_cz],
         w[:, o_cdt:o_cdt + H_C]], axis=1)
    small = jnp.pad(small, ((0, 0), (0, LANES - small.shape[1])))
    return jnp.concatenate(
        [w[:, :o_aik], small, w[:, o_bqkv:o_bbeta], w[:, o_cz:o_cdt]], axis=1).astype(MM)


def _inproj(h, g, w_arr, qn, kn, tm):
    n, d = h.shape
    wtot = w_arr.shape[1]
    bd = _block_diag(W_A, DH_A, 1.0 / DH_A)
    row = lambda w: pl.BlockSpec((tm, w), lambda i: (i, 0))
    outs = [(W_A, MM), (W_A, F32), (W_A, MM), (W_A, F32), (W_A, MM), (W_A, MM),
            (LANES, F32), (LANES, MM), (CB, F32), (W_B, F32), (W_C, F32), (CC, F32)]
    return pl.pallas_call(
        _inproj_kernel,
        grid=(n // tm,),
        in_specs=[row(d), _full((1, d)), _full((d, wtot)), _full((1, W_A)), _full((1, W_A)),
                  _full((W_A, W_A))],
        out_specs=[row(w) for w, _ in outs],
        out_shape=[jax.ShapeDtypeStruct((n, w), t) for w, t in outs],
        compiler_params=_params(("parallel",)),
        name="inproj",
    )(h, g.reshape(1, d), w_arr, jnp.tile(qn, H_A).reshape(1, W_A), jnp.tile(kn, H_A).reshape(1, W_A), bd)


def _mix_prologue(h_ref, ya_ref, yb_ref, yc_ref, wo_ref, gf_ref, h1_ref, u2_ref, acc_ref):
    h1 = h_ref[...] + (_dot(ya_ref[...], wo_ref[0:W_A, :])
                       + _dot(yb_ref[...], wo_ref[W_A:W_A + W_B, :])
                       + _dot(yc_ref[...], wo_ref[W_A + W_B:, :]))
    h1_ref[...] = h1
    u2 = _rms(h1, gf_ref[...])
    u2_ref[...] = u2.astype(MM)
    acc_ref[...] = jnp.zeros_like(acc_ref)
    return u2


def _mix_epilogue(p_ref, gp_ref, wg_ref, wp_ref, h1_ref, acc_ref, o_ref):
    h2 = h1_ref[...] + acc_ref[...]
    u3 = _rms(h2, gp_ref[...]).astype(MM)
    gate = _sigmoid(_dot(u3, wg_ref[...]))
    o_ref[...] = h2 + gate * _dot(p_ref[...].astype(MM), wp_ref[...])


def _ffn_kernel(h_ref, ya_ref, yb_ref, yc_ref, p_ref, wo_ref, gf_ref, w1_ref, w3_ref, w2_ref,
                gp_ref, wg_ref, wp_ref, o_ref, h1_ref, u2_ref, acc_ref):
    f = pl.program_id(1)

    @pl.when(f == 0)
    def _():
        _mix_prologue(h_ref, ya_ref, yb_ref, yc_ref, wo_ref, gf_ref, h1_ref, u2_ref, acc_ref)

    u2 = u2_ref[...]
    hid = _silu(_dot(u2, w1_ref[...])) * _dot(u2, w3_ref[...])
    acc_ref[...] += _dot(hid.astype(MM), w2_ref[...])

    @pl.when(f == pl.num_programs(1) - 1)
    def _():
        _mix_epilogue(p_ref, gp_ref, wg_ref, wp_ref, h1_ref, acc_ref, o_ref)


def _moe_kernel(h_ref, ya_ref, yb_ref, yc_ref, p_ref, wo_ref, gf_ref, wr_ref, w1_ref, w3_ref, w2_ref,
                gp_ref, wg_ref, wp_ref, o_ref, h1_ref, u2_ref, acc_ref, comb_ref):
    e = pl.program_id(1)
    f = pl.program_id(2)

    @pl.when((e == 0) & (f == 0))
    def _():
        u2 = _mix_prologue(h_ref, ya_ref, yb_ref, yc_ref, wo_ref, gf_ref, h1_ref, u2_ref, acc_ref)
        logits = _dot_hi(u2, wr_ref[...])
        lane = lax.broadcasted_iota(I32, logits.shape, 1)
        big = jnp.int32(LANES)
        lg = jnp.where(lane < N_EXPERTS, logits, -jnp.inf)
        m1 = jnp.max(lg, axis=1, keepdims=True)
        i1 = jnp.min(jnp.where(lg == m1, lane, big), axis=1, keepdims=True)
        lg2 = jnp.where(lane == i1, -jnp.inf, lg)
        m2 = jnp.max(lg2, axis=1, keepdims=True)
        i2 = jnp.min(jnp.where(lg2 == m2, lane, big), axis=1, keepdims=True)
        e2 = jnp.exp(m2 - m1)
        g1 = 1.0 / (1.0 + e2)
        g2 = e2 / (1.0 + e2)
        for ex in range(N_EXPERTS):
            ce = jnp.where(i1 == ex, g1, 0.0) + jnp.where(i2 == ex, g2, 0.0)
            comb_ref[ex] = jnp.broadcast_to(ce, comb_ref.shape[1:])

    u2 = u2_ref[...]
    hid = _silu(_dot(u2, w1_ref[0])) * _dot(u2, w3_ref[0])
    tf = hid.shape[1]
    ce = comb_ref[e]
    hid = hid * jnp.concatenate([ce] * (tf // LANES), axis=1)
    acc_ref[...] += _dot(hid.astype(MM), w2_ref[0])

    @pl.when((e == pl.num_programs(1) - 1) & (f == pl.num_programs(2) - 1))
    def _():
        _mix_epilogue(p_ref, gp_ref, wg_ref, wp_ref, h1_ref, acc_ref, o_ref)


def _channel_mix(h, ya, yb, yc, p, wo, gf, gp, wg, wp, tm, tf, *, ffn=None, moe=None):
    n, d = h.shape
    dple = p.shape[1]
    if ffn is not None:
        w1, w3, w2 = ffn
        dff = w1.shape[1]
        grid = (n // tm, dff // tf)
        row = lambda w: pl.BlockSpec((tm, w), lambda i, f: (i, 0))
        wspecs = [pl.BlockSpec((d, tf), lambda i, f: (0, f)), pl.BlockSpec((d, tf), lambda i, f: (0, f)),
                  pl.BlockSpec((tf, d), lambda i, f: (f, 0))]
        wargs = [w1, w3, w2]
        body = _ffn_kernel
        extra_scratch = []
        sem = ("parallel", "arbitrary")
    else:
        wr, w1, w3, w2 = moe
        ne, _, dff = w1.shape
        grid = (n // tm, ne, dff // tf)
        row = lambda w: pl.BlockSpec((tm, w), lambda i, e, f: (i, 0))
        wspecs = [_full((d, LANES)),
                  pl.BlockSpec((1, d, tf), lambda i, e, f: (e, 0, f)),
                  pl.BlockSpec((1, d, tf), lambda i, e, f: (e, 0, f)),
                  pl.BlockSpec((1, tf, d), lambda i, e, f: (e, f, 0))]
        wargs = [jnp.pad(wr, ((0, 0), (0, LANES - ne))), w1, w3, w2]
        body = _moe_kernel
        extra_scratch = [pltpu.VMEM((ne, tm, LANES), F32)]
        sem = ("parallel", "arbitrary", "arbitrary")
    return pl.pallas_call(
        body,
        grid=grid,
        in_specs=[row(d), row(W_A), row(W_B), row(W_C), row(dple), _full((d, d)), _full((1, d))]
                 + wspecs + [_full((1, d)), _full((d, d)), _full((dple, d))],
        out_specs=row(d),
        out_shape=jax.ShapeDtypeStruct((n, d), F32),
        scratch_shapes=[pltpu.VMEM((tm, d), F32), pltpu.VMEM((tm, d), MM), pltpu.VMEM((tm, d), F32)]
                       + extra_scratch,
        compiler_params=_params(sem),
        name="ffn" if ffn is not None else "moe",
    )(h, ya, yb, yc, p, wo, gf.reshape(1, d), *wargs, gp.reshape(1, d), wg, wp)


def _sortable(x):
    x = jnp.where(x == 0.0, 0.0, x)
    b = lax.bitcast_convert_type(x, I32)
    return b ^ ((b >> 31) & jnp.int32(0x7FFFFFFF))


def _select_topk(keys_ref, bias_ref, n_tiles, topk, idx_bits, shape):
    zero = jnp.zeros(shape, I32)
    lane = lax.broadcasted_iota(I32, shape, 1)
    imin = jnp.int32(INT_MIN)

    def count(pred):
        def body(j, acc):
            return acc + jnp.where(pred(keys_ref[j], j), 1, 0)
        acc = lax.fori_loop(0, n_tiles, body, zero)
        return jnp.broadcast_to(jnp.sum(acc, axis=1, keepdims=True), shape)

    def value_bit(b, u):
        uc = u | jnp.left_shift(jnp.int32(1), 31 - b)
        cand = uc ^ imin
        cnt = count(lambda k, j: k >= cand)
        return jnp.where(cnt >= topk, uc, u)

    u = lax.fori_loop(0, 32, value_bit, zero)
    thr = jnp.maximum(u ^ imin, imin + 1)
    need = topk - count(lambda k, j: k > thr)

    def index_bit(b, c):
        cc = c | jnp.left_shift(jnp.int32(1), idx_bits - 1 - b)
        cnt = count(lambda k, j: (k == thr) & (j * LANES + lane < cc))
        return jnp.where(cnt <= need, cc, c)

    cut = lax.fori_loop(0, idx_bits, index_bit, zero)

    def write(j, carry):
        k = keys_ref[j]
        sel = (k > thr) | ((k == thr) & (j * LANES + lane < cut))
        bias_ref[j] = jnp.where(sel, 0.0, NEG)
        return carry

    lax.fori_loop(0, n_tiles, write, 0)


def _dsa_prompt_kernel(q_ref, qi_ref, sm_ref, k_ref, v_ref, ki_ref, o_ref,
                       keys_ref, bias_ref, wb_ref, qp_ref, qip_ref, mx_ref, ls_ref, acc_ref,
                       *, topk, kc, idx_bits):
    i = pl.program_id(1)
    qb = q_ref.shape[0]
    tpc = kc // LANES
    n_chunks = ((i + 1) * qb + kc - 1) // kc
    n_tiles = n_chunks * tpc
    shape = (qb, LANES)
    lane = lax.broadcasted_iota(I32, shape, 1)
    pos = i * qb + lax.broadcasted_iota(I32, shape, 0)
    low = lane < DH_A

    wi = sm_ref[:, SM_WI:SM_WI + H_I] * ((D_I ** -0.5) * (H_I ** -0.5))
    for h in range(H_I):
        wb_ref[h] = jnp.broadcast_to(wi[:, h:h + 1], shape)
    for h in range(H_A):
        pr = h // 2
        blk = q_ref[:, pr * LANES:(pr + 1) * LANES]
        qp_ref[h] = jnp.where(low if h % 2 == 0 else ~low, blk, jnp.zeros_like(blk))
        blk = qi_ref[:, pr * LANES:(pr + 1) * LANES]
        if h % 2 == 1:
            blk = pltpu.roll(blk, DH_A, 1)
        qip_ref[h] = jnp.where(low, blk, jnp.zeros_like(blk))

    def score_chunk(c, carry):
        for t in range(tpc):
            j = c * tpc + t
            base = pl.multiple_of(j * LANES, LANES)
            kt = ki_ref[pl.ds(base, LANES), :]
            sc = jnp.zeros(shape, F32)
            for h in range(H_I):
                sc = sc + wb_ref[h] * jnp.maximum(_dot_nt(qip_ref[h], kt), 0.0)
            keys_ref[j] = jnp.where(base + lane <= pos, _sortable(sc), jnp.int32(INT_MIN))
        return carry

    lax.fori_loop(0, n_chunks, score_chunk, 0)
    _select_topk(keys_ref, bias_ref, n_tiles, topk, idx_bits, shape)

    for h in range(H_A):
        mx_ref[h] = jnp.full(shape, NEG, F32)
        ls_ref[h] = jnp.zeros(shape, F32)
        acc_ref[h] = jnp.zeros(shape, F32)

    def logits(c, h):
        base = pl.multiple_of(c * kc, kc)
        pr = h // 2
        kk = k_ref[pl.ds(base, kc), pr * LANES:(pr + 1) * LANES]
        s = _dot_nt(qp_ref[h], kk)
        return [s[:, t * LANES:(t + 1) * LANES] + bias_ref[c * tpc + t] for t in range(tpc)]

    def max_chunk(c, carry):
        for h in range(H_A):
            m = functools.reduce(jnp.maximum, logits(c, h))
            mx_ref[h] = jnp.maximum(mx_ref[h], m)
        return carry

    lax.fori_loop(0, n_chunks, max_chunk, 0)
    for h in range(H_A):
        mx_ref[h] = jnp.broadcast_to(jnp.max(mx_ref[h], axis=1, keepdims=True), shape)

    def pv_chunk(c, carry):
        base = pl.multiple_of(c * kc, kc)
        for h in range(H_A):
            pr = h // 2
            m = mx_ref[h]
            ps = [jnp.exp(s - m) for s in logits(c, h)]
            ls_ref[h] += functools.reduce(jnp.add, ps)
            p = jnp.concatenate(ps, axis=1).astype(MM)
            acc_ref[h] += _dot(p, v_ref[pl.ds(base, kc), pr * LANES:(pr + 1) * LANES])
        return carry

    lax.fori_loop(0, n_chunks, pv_chunk, 0)
    for pr in range(H_A // 2):
        outs = []
        for h in (2 * pr, 2 * pr + 1):
            l = jnp.sum(ls_ref[h], axis=1, keepdims=True)
            outs.append(acc_ref[h] / l)
        o_ref[:, pr * LANES:(pr + 1) * LANES] = jnp.where(low, outs[0], outs[1]).astype(o_ref.dtype)


def _dsa_prompt(q16, qi16, sm32, k16, v16, sm16, batch, seq):
    qb = Q_BLOCK
    topk = min(TOPK, seq // 4)
    kc = math.gcd(seq, 512)
    nq = seq // qb
    idx_bits = int(seq).bit_length() + 1
    blk = lambda w: pl.BlockSpec((qb, w), lambda b, i: (b * nq + i, 0))
    seqblk = lambda w: pl.BlockSpec((seq, w), lambda b, i: (b, 0), pipeline_mode=pl.Buffered(1))
    tiles = (seq // LANES, qb, LANES)
    return pl.pallas_call(
        functools.partial(_dsa_prompt_kernel, topk=topk, kc=kc, idx_bits=idx_bits),
        grid=(batch, nq),
        in_specs=[blk(W_A), blk(W_A), blk(LANES), seqblk(W_A), seqblk(W_A), seqblk(LANES)],
        out_specs=blk(W_A),
        out_shape=jax.ShapeDtypeStruct((batch * seq, W_A), MM),
        scratch_shapes=[pltpu.VMEM(tiles, I32), pltpu.VMEM(tiles, F32),
                        pltpu.VMEM((H_I, qb, LANES), F32), pltpu.VMEM((H_A, qb, LANES), MM),
                        pltpu.VMEM((H_I, qb, LANES), MM), pltpu.VMEM((H_A, qb, LANES), F32),
                        pltpu.VMEM((H_A, qb, LANES), F32), pltpu.VMEM((H_A, qb, LANES), F32)],
        compiler_params=_params(("parallel", "arbitrary")),
        name="dsa_prompt",
    )(q16, qi16, sm32, k16, v16, sm16)


def _expand_matrix(first_lane, heads, width):
    m = np.zeros((LANES, heads * width), np.float32)
    for h in range(heads):
        m[first_lane + h, h * width:(h + 1) * width] = 1.0
    return jnp.asarray(m, MM)


def _chunk_masks(tt, chunk):
    i = np.arange(tt)
    same = (i[:, None] // chunk) == (i[None, :] // chunk)
    ltri = jnp.asarray(np.where(same & (i[:, None] >= i[None, :]), 1.0, 0.0), MM)
    utri = jnp.asarray(np.where(same & (i[:, None] <= i[None, :]), 1.0, 0.0), F32)
    return ltri, utri


def _lane_vec(values, first_lane):
    return jnp.zeros((1, LANES), F32).at[0, first_lane:first_lane + values.shape[0]].set(values)


def _conv_tile(x_ref, buf_ref, cw_ref, xe_ref, nb_ref, t, n_t, tv_last):
    tt = x_ref.shape[0]
    hist = SUBLANES

    @pl.when(t == 0)
    def _():
        xe_ref[0:hist, :] = jnp.zeros((hist, xe_ref.shape[1]), F32)
        xe_ref[hist - (CONV_W - 1):hist, :] = buf_ref[0]

    xe_ref[hist:hist + tt, :] = x_ref[...]
    y = jnp.zeros(x_ref.shape, F32)
    for i in range(CONV_W):
        o = hist - (CONV_W - 1) + i
        y = y + xe_ref[o:o + tt, :] * cw_ref[i:i + 1, :]

    @pl.when(t == n_t - 1)
    def _():
        nb_ref[0] = xe_ref[hist + tv_last - (CONV_W - 1):hist + tv_last, :]

    xe_ref[0:hist, :] = xe_ref[tt:tt + hist, :]
    return y


def _live_rows(shape, t, n_t, tv_last):
    row = lax.broadcasted_iota(I32, shape, 0)
    return (t < n_t - 1) | (row < tv_last)


def _row_cumsum(vals, utri, ones_c, heads, width, chunk):
    tt = vals.shape[0]
    out = []
    for h in range(heads):
        col = vals[:, h * width:h * width + chunk]
        out.append(_x3_dot(ones_c, jnp.concatenate([col] * (tt // chunk), axis=1) * utri))
    return out


def _gdn_kernel(x_ref, z_ref, sm_ref, buf_ref, s0_ref, cw_ref, alog_ref, dtb_ref, ng_ref,
                ebeta_ref, eg_ref, ltri_ref, utri_ref, bd1_ref, bdn_ref,
                y_ref, nb_ref, st_ref, xe_ref, s_ref, o_ref, *, tv_last, chunk):
    t = pl.program_id(1)
    n_t = pl.num_programs(1)
    tt = x_ref.shape[0]
    wq = H_B * DK_B

    @pl.when(t == 0)
    def _():
        s_ref[...] = s0_ref[0]

    qkv = _silu(_conv_tile(x_ref, buf_ref, cw_ref, xe_ref, nb_ref, t, n_t, tv_last))
    qk = qkv[:, :2 * wq]
    qk = qk * lax.rsqrt(_group_mean(qk * qk, bd1_ref[...]) + EPS)
    q = qk[:, :wq] * (DK_B ** -0.5)
    k = qk[:, wq:]
    v = qkv[:, 2 * wq:]

    sm = sm_ref[...]
    beta = _sigmoid(sm)
    g = -jnp.exp(alog_ref[...]) * _softplus(sm + dtb_ref[...])
    if tv_last < tt:
        live = _live_rows(sm.shape, t, n_t, tv_last)
        beta = jnp.where(live, beta, 0.0)
        g = jnp.where(live, g, 0.0)
    betab = _dot_x3(beta, ebeta_ref[...])
    gb = _dot_x3(g, eg_ref[...])
    gc = _x3_dot(ltri_ref[...], gb)
    ones_c = jnp.ones((chunk, tt), MM)
    gr = _row_cumsum(gb, utri_ref[...], ones_c, H_B, DK_B, chunk)
    eg = jnp.exp(gc)
    kb = k * betab
    vb = v * betab
    kbeg = kb * eg
    qg = q * eg
    ii = lax.broadcasted_iota(I32, (chunk, chunk), 0)
    jj = lax.broadcasted_iota(I32, (chunk, chunk), 1)
    levels = max(1, (chunk - 1).bit_length())

    intra = {}
    for c in range(tt // chunk):
        rs = slice(c * chunk, (c + 1) * chunk)
        for h in range(H_B):
            hs = slice(h * DK_B, (h + 1) * DK_B)
            diff = gc[rs, hs] - gr[h][:, rs]
            decay = jnp.exp(jnp.where(ii >= jj, diff, NEG))
            k16 = k[rs, hs].astype(MM)
            a = jnp.where(ii > jj, _dot_nt(kb[rs, hs].astype(MM), k16) * decay, 0.0)
            attn = _dot_nt(q[rs, hs].astype(MM), k16) * decay
            x = jnp.concatenate([vb[rs, hs], kbeg[rs, hs]], axis=1)
            p = -a
            for lv in range(levels):
                if lv < levels - 1:
                    xp = _dot_hi(p, jnp.concatenate([x, p], axis=1))
                    x = x + xp[:, :x.shape[1]]
                    p = xp[:, x.shape[1]:]
                else:
                    x = x + _dot_hi(p, x)
            intra[c, h] = (x[:, :DV_B], x[:, DV_B:], attn)

    for c in range(tt // chunk):
        rs = slice(c * chunk, (c + 1) * chunk)
        last = gc[(c + 1) * chunk - 1:(c + 1) * chunk, :]
        kg = k[rs, :] * jnp.exp(last - gc[rs, :])
        gl = jnp.exp(last)
        for h in range(H_B):
            hs = slice(h * DK_B, (h + 1) * DK_B)
            u0, w, attn = intra[c, h]
            s = s_ref[h]
            s16 = s.astype(MM)
            u = u0 - _dot(w.astype(MM), s16)
            u16 = u.astype(MM)
            o_ref[rs, hs] = _dot(qg[rs, hs].astype(MM), s16) + _dot(attn.astype(MM), u16)
            s_ref[h] = s * gl[:, h * DK_B:h * DK_B + 1] + _dot_tn(kg[:, hs].astype(MM), u16)

    o = o_ref[...]
    o = o * lax.rsqrt(_group_mean(o * o, bdn_ref[...]) + EPS) * ng_ref[...]
    y_ref[...] = (o * _silu(z_ref[...])).astype(y_ref.dtype)

    @pl.when(t == n_t - 1)
    def _():
        st_ref[0] = s_ref[...]


def _ssd_kernel(x_ref, z_ref, sm_ref, buf_ref, h0_ref, cw_ref, cb_ref, alog_ref, dtb_ref, d_ref, ng_ref,
                edt_ref, ltri_ref, utri_ref, bdn_ref,
                y_ref, nb_ref, ht_ref, xe_ref, h_ref, o_ref, *, tv_last, chunk):
    t = pl.program_id(1)
    n_t = pl.num_programs(1)
    tt = x_ref.shape[0]
    gw = G_C * N_C
    hpg = H_C // G_C

    @pl.when(t == 0)
    def _():
        h_ref[...] = h0_ref[0]

    xbc = _silu(_conv_tile(x_ref, buf_ref, cw_ref, xe_ref, nb_ref, t, n_t, tv_last) + cb_ref[...])
    xs = xbc[:, :W_C]
    bm = xbc[:, W_C:W_C + gw].astype(MM)
    cm = xbc[:, W_C + gw:].astype(MM)

    dt = _softplus(sm_ref[...] + dtb_ref[...])
    if tv_last < tt:
        dt = jnp.where(_live_rows(dt.shape, t, n_t, tv_last), dt, 0.0)
    dtb = _dot_x3(dt, edt_ref[...])
    a = dtb * (-jnp.exp(alog_ref[...]))
    xdt = xs * dtb
    ac = _x3_dot(ltri_ref[...], a)
    ones_c = jnp.ones((chunk, tt), MM)
    ar = _row_cumsum(a, utri_ref[...], ones_c, H_C, P_C, chunk)
    ea = jnp.exp(ac)
    ii = lax.broadcasted_iota(I32, (chunk, chunk), 0)
    jj = lax.broadcasted_iota(I32, (chunk, chunk), 1)

    for c in range(tt // chunk):
        rs = slice(c * chunk, (c + 1) * chunk)
        last = ac[(c + 1) * chunk - 1:(c + 1) * chunk, :]
        xdec = (xdt[rs, :] * jnp.exp(last - ac[rs, :])).astype(MM)
        hdec = jnp.exp(last)
        for g in range(G_C):
            gs = slice(g * N_C, (g + 1) * N_C)
            cbm = _dot_nt(cm[rs, gs], bm[rs, gs])
            for h in range(g * hpg, (g + 1) * hpg):
                hs = slice(h * P_C, (h + 1) * P_C)
                lmat = jnp.exp(jnp.where(ii >= jj, ac[rs, hs] - ar[h][:, rs], NEG))
                hprev = h_ref[h]
                y = _dot((cbm * lmat).astype(MM), xdt[rs, hs].astype(MM))
                y = y + ea[rs, hs] * _dot_nt(cm[rs, gs], hprev.astype(MM))
                o_ref[rs, hs] = y + d_ref[:, hs] * xs[rs, hs]
                h_ref[h] = hprev * hdec[:, h * P_C:h * P_C + 1] + _dot_tn(xdec[:, hs], bm[rs, gs])

    yz = o_ref[...] * _silu(z_ref[...])
    y_ref[...] = (yz * lax.rsqrt(_group_mean(yz * yz, bdn_ref[...]) + EPS) * ng_ref[...]).astype(y_ref.dtype)

    @pl.when(t == n_t - 1)
    def _():
        ht_ref[0] = h_ref[...]


def _recurrent_call(body, name, x, z, sm, buf, state0, vecs, consts, batch, t_pad, tt, tv_last, width_out):
    n_t = t_pad // tt
    cw = x.shape[1]
    blk = lambda w: pl.BlockSpec((tt, w), lambda b, t: (b * n_t + t, 0))
    per_b = lambda shp: pl.BlockSpec((1,) + shp, lambda b, t: (b,) + (0,) * len(shp))
    sshape = state0.shape[1:]
    return pl.pallas_call(
        functools.partial(body, tv_last=tv_last, chunk=min(CHUNK, tt)),
        grid=(batch, n_t),
        in_specs=[blk(cw), blk(width_out), blk(LANES), per_b((CONV_W - 1, cw)), per_b(sshape)]
                 + [_full(v.shape) for v in vecs] + [_full(c.shape) for c in consts],
        out_specs=[blk(width_out), per_b((CONV_W - 1, cw)), per_b(sshape)],
        out_shape=[jax.ShapeDtypeStruct((batch * t_pad, width_out), MM),
                   jax.ShapeDtypeStruct((batch, CONV_W - 1, cw), F32),
                   jax.ShapeDtypeStruct((batch,) + sshape, F32)],
        scratch_shapes=[pltpu.VMEM((tt + SUBLANES, cw), F32), pltpu.VMEM(sshape, F32),
                        pltpu.VMEM((tt, width_out), F32)],
        compiler_params=_params(("parallel", "arbitrary")),
        name=name,
    )(x, z, sm, buf, state0, *vecs, *consts)


def _gdn(x, z, sm, buf, s0, conv_w, a_log, dt_bias, norm_g, batch, t_pad, tt, tv_last):
    ltri, utri = _chunk_masks(tt, min(CHUNK, tt))
    vecs = [conv_w, _lane_vec(a_log, SM_A), _lane_vec(dt_bias, SM_A), jnp.tile(norm_g, H_B).reshape(1, W_B)]
    consts = [_expand_matrix(SM_BETA, H_B, DK_B), _expand_matrix(SM_A, H_B, DK_B), ltri, utri,
              _block_diag(2 * H_B * DK_B, DK_B, 1.0), _block_diag(W_B, DV_B, 1.0 / DV_B)]
    return _recurrent_call(_gdn_kernel, "gdn", x, z, sm, buf, s0, vecs, consts, batch, t_pad, tt, tv_last, W_B)


def _ssd(x, z, sm, buf, h0, conv_w, conv_b, a_log, dt_bias, dskip, norm_g, batch, t_pad, tt, tv_last):
    ltri, utri = _chunk_masks(tt, min(CHUNK, tt))
    vecs = [conv_w, conv_b.reshape(1, CC), jnp.repeat(a_log, P_C).reshape(1, W_C), _lane_vec(dt_bias, SM_DT),
            jnp.repeat(dskip, P_C).reshape(1, W_C), norm_g.reshape(1, W_C)]
    consts = [_expand_matrix(SM_DT, H_C, P_C), ltri, utri, _block_diag(W_C, W_C // G_C, float(G_C) / W_C)]
    return _recurrent_call(_ssd_kernel, "ssd", x, z, sm, buf, h0, vecs, consts, batch, t_pad, tt, tv_last, W_C)


def _dsa_sample_select_kernel(pt_ref, qi_ref, w_ref, kin_ref, page_ref, bias_ref, keys_ref,
                              *, topk, n_q, idx_bits):
    j = pl.program_id(1)
    n_p = pl.num_programs(1)
    shape = (SUBLANES, LANES)
    row = lax.broadcasted_iota(I32, shape, 0)
    lane = lax.broadcasted_iota(I32, shape, 1)
    imin = jnp.int32(INT_MIN)

    def keys_of(kt, valid):
        r = jnp.maximum(_dot(qi_ref[0], kt), 0.0) * w_ref[0]
        per_q = [jnp.sum(r[q * H_I:(q + 1) * H_I], axis=0, keepdims=True) for q in range(n_q)]
        sc = jnp.concatenate(per_q + [jnp.zeros((SUBLANES - n_q, LANES), F32)], axis=0)
        return jnp.where(valid, _sortable(sc), imin)

    keys_ref[j] = keys_of(page_ref[0, 0].astype(MM), row < n_q)

    @pl.when(j == n_p - 1)
    def _():
        keys_ref[n_p] = keys_of(kin_ref[0], (row < n_q) & (lane <= row))
        _select_topk(keys_ref, bias_ref.at[0], n_p + 1, topk, idx_bits, shape)


def _dsa_sample_attend_kernel(pt_ref, q_ref, bias_ref, bnew_ref, kp_ref, vp_ref, kn_ref, vn_ref, o_ref,
                              qp_ref, m_ref, l_ref, acc_ref):
    j = pl.program_id(1)
    n_p = pl.num_programs(1)
    shape = (SUBLANES, LANES)
    lane = lax.broadcasted_iota(I32, shape, 1)
    low = lane < DH_A

    @pl.when(j == 0)
    def _():
        for h in range(H_A):
            blk = q_ref[0][:, (h // 2) * LANES:(h // 2 + 1) * LANES]
            qp_ref[h] = jnp.where(low if h % 2 == 0 else ~low, blk, jnp.zeros_like(blk))
            m_ref[h] = jnp.full(shape, NEG, F32)
            l_ref[h] = jnp.zeros(shape, F32)
            acc_ref[h] = jnp.zeros(shape, F32)

    def absorb(kk, vv, bias):
        sel = bias == 0.0
        for h in range(H_A):
            ps = slice((h // 2) * LANES, (h // 2 + 1) * LANES)
            s = _dot(qp_ref[h], kk[ps, :]) + bias
            m_old = m_ref[h]
            m_new = jnp.maximum(m_old, jnp.broadcast_to(jnp.max(s, axis=1, keepdims=True), shape))
            alpha = jnp.exp(m_old - m_new)
            p = jnp.where(sel, jnp.exp(s - m_new), 0.0)
            l_ref[h] = l_ref[h] * alpha + jnp.broadcast_to(jnp.sum(p, axis=1, keepdims=True), shape)
            acc_ref[h] = acc_ref[h] * alpha + _dot_nt(p.astype(MM), vv[ps, :])
            m_ref[h] = m_new

    absorb(kp_ref[0, 0].astype(MM), vp_ref[0, 0].astype(MM), bias_ref[0, 0])

    @pl.when(j == n_p - 1)
    def _():
        absorb(kn_ref[0], vn_ref[0], bnew_ref[0, 0])
        for pr in range(H_A // 2):
            outs = [acc_ref[h] / l_ref[h] for h in (2 * pr, 2 * pr + 1)]
            o_ref[0, :, pr * LANES:(pr + 1) * LANES] = jnp.where(low, outs[0], outs[1]).astype(o_ref.dtype)


def _dsa_sample(layer, q16, qi16, sm32, sm16, k16, v16, cache_k, cache_v, cache_ki, page_table, bs, ts):
    n_pages = page_table.shape[1]
    page = cache_ki.shape[2]
    assert page == LANES and ts <= SUBLANES
    past = n_pages * page
    topk = min(TOPK, (past + ts) // 4)
    idx_bits = int(past + ts).bit_length() + 1
    pt = page_table.reshape(-1)
    rq = ts * H_I
    qi_r = qi16.reshape(bs, rq, D_I)
    wi = sm32[:, SM_WI:SM_WI + H_I] * ((D_I ** -0.5) * (H_I ** -0.5))
    w_r = jnp.broadcast_to(wi.reshape(bs, rq, 1), (bs, rq, LANES))
    pad_rows = lambda a, rows: jnp.pad(a.reshape(bs, ts, -1), ((0, 0), (0, rows - ts), (0, 0)))
    new_t = lambda a: jnp.pad(jnp.swapaxes(a.reshape(bs, ts, -1), 1, 2), ((0, 0), (0, 0), (0, page - ts)))
    kin = new_t(sm16[:, SM_KI:SM_KI + D_I])
    per_b = lambda shp: pl.BlockSpec((1,) + shp, lambda b, j, p: (b,) + (0,) * len(shp))
    paged = lambda w: pl.BlockSpec((1, 1, w, page), lambda b, j, p: (layer, p[b * n_pages + j], 0, 0))
    tiles = n_pages + 1

    bias = pl.pallas_call(
        functools.partial(_dsa_sample_select_kernel, topk=topk, n_q=ts, idx_bits=idx_bits),
        grid_spec=pltpu.PrefetchScalarGridSpec(
            num_scalar_prefetch=1, grid=(bs, n_pages),
            in_specs=[per_b((rq, D_I)), per_b((rq, LANES)), per_b((D_I, page)), paged(D_I)],
            out_specs=per_b((tiles, SUBLANES, LANES)),
            scratch_shapes=[pltpu.VMEM((tiles, SUBLANES, LANES), I32)]),
        out_shape=jax.ShapeDtypeStruct((bs, tiles, SUBLANES, LANES), F32),
        compiler_params=_params(("parallel", "arbitrary")),
        name="dsa_sample_select",
    )(pt, qi_r, w_r, kin, jnp.swapaxes(cache_ki, 2, 3))

    page_t = lambda c: jnp.transpose(c, (0, 1, 3, 4, 2)).reshape(c.shape[:2] + (W_A, page))
    ck = page_t(cache_k)
    cv = page_t(cache_v)
    hs = pltpu.VMEM((H_A, SUBLANES, LANES), F32)
    out = pl.pallas_call(
        _dsa_sample_attend_kernel,
        grid_spec=pltpu.PrefetchScalarGridSpec(
            num_scalar_prefetch=1, grid=(bs, n_pages),
            in_specs=[per_b((SUBLANES, W_A)),
                      pl.BlockSpec((1, 1, SUBLANES, LANES), lambda b, j, p: (b, j, 0, 0)),
                      pl.BlockSpec((1, 1, SUBLANES, LANES), lambda b, j, p: (b, n_pages, 0, 0)),
                      paged(W_A), paged(W_A), per_b((W_A, page)), per_b((W_A, page))],
            out_specs=per_b((SUBLANES, W_A)),
            scratch_shapes=[pltpu.VMEM((H_A, SUBLANES, LANES), MM), hs, hs, hs]),
        out_shape=jax.ShapeDtypeStruct((bs, SUBLANES, W_A), MM),
        compiler_params=_params(("parallel", "arbitrary")),
        name="dsa_sample_attend",
    )(pt, pad_rows(q16, SUBLANES), bias, bias, ck, cv, new_t(k16), new_t(v16))
    return out[:, :ts].reshape(bs * ts, W_A)


def _pad_seq(a, batch, t, t_pad):
    if t == t_pad:
        return a
    return jnp.pad(a.reshape(batch, t, -1), ((0, 0), (0, t_pad - t), (0, 0))).reshape(batch * t_pad, -1)


def _trunk(x, p, lw, conv_b0, ssm_b0, conv_c0, ssm_c0, kv, tm, tt):
    batch, t, d = x.shape
    n = batch * t
    depth = len(lw)
    t_pad = -(-t // tt) * tt
    tv_last = t - (t_pad - tt)
    assert min(CHUNK, tt) == DK_B == P_C and tv_last >= CONV_W - 1
    h = x.reshape(n, d)
    rec = [[] for _ in range(7)]
    for l, w in enumerate(lw):
        (q16, k32, k16, v32, v16, qi16, sm32, sm16, bqkv, bz, cz, cxbc) = _inproj(
            h, w["g_mix"], w["w_in"], w["q_norm"], w["k_norm"], tm)
        if kv is None:
            ya = _dsa_prompt(q16, qi16, sm32, k16, v16, sm16, batch, t)
        else:
            ya = _dsa_sample(l, q16, qi16, sm32, sm16, k16, v16, kv[0], kv[1], kv[2], kv[3], batch, t)
        pad = lambda a: _pad_seq(a, batch, t, t_pad)
        unpad = lambda a: a if t == t_pad else a.reshape(batch, t_pad, -1)[:, :t].reshape(n, -1)
        smp = pad(sm32)
        yb, bconv, bs_ = _gdn(pad(bqkv), pad(bz), smp, conv_b0[l], ssm_b0[l], w["b_conv_w"], w["b_A_log"],
                              w["b_dt_bias"], w["b_norm"], batch, t_pad, tt, tv_last)
        yc, cconv, cs_ = _ssd(pad(cxbc), pad(cz), smp, conv_c0[l], ssm_c0[l], w["c_conv_w"], w["c_conv_b"],
                              w["c_A_log"], w["c_dt_bias"], w["c_D"], w["c_norm"], batch, t_pad, tt, tv_last)
        h = _channel_mix(h, ya, unpad(yb), unpad(yc), p[l].reshape(n, -1), w["w_out"], w["g_ffn"], w["g_ple"],
                         w["w_ple_gate"], w["w_ple_proj"], tm, 512, ffn=w.get("ffn"), moe=w.get("moe"))
        for r, a in zip(rec, (k32.reshape(batch, t, H_A, DH_A), v32.reshape(batch, t, H_A, DH_A),
                              sm32[:, SM_KI:SM_KI + D_I].reshape(batch, t, D_I), bconv, bs_, cconv, cs_)):
            r.append(a)
    return h.reshape(batch, t, d), [jnp.stack(r) for r in rec]


def kernel(x_prompt, x_sample, cache_k, cache_v, cache_kidx, state_b_conv, state_b_ssm, state_c_conv, state_c_ssm, page_table, p_prompt, p_sample, w_in, w_out, g_mix, g_ffn, g_ple, q_norm, k_norm, b_conv_w, b_A_log, b_dt_bias, b_norm, c_conv_w, c_conv_b, c_A_log, c_dt_bias, c_D, c_norm, ffn_w1, ffn_w3, ffn_w2, moe_router, moe_w1, moe_w3, moe_w2, w_ple_gate, w_ple_proj):
    depth = w_in.shape[0]
    lw = []
    for l in range(depth):
        w = dict(w_in=_arrange_w_in(w_in[l]), w_out=w_out[l].astype(MM), g_mix=g_mix[l], g_ffn=g_ffn[l],
                 g_ple=g_ple[l], q_norm=q_norm[l], k_norm=k_norm[l], b_conv_w=b_conv_w[l], b_A_log=b_A_log[l],
                 b_dt_bias=b_dt_bias[l], b_norm=b_norm[l], c_conv_w=c_conv_w[l], c_conv_b=c_conv_b[l],
                 c_A_log=c_A_log[l], c_dt_bias=c_dt_bias[l], c_D=c_D[l], c_norm=c_norm[l],
                 w_ple_gate=w_ple_gate[l].astype(MM), w_ple_proj=w_ple_proj[l].astype(MM))
        j = l // 2
        if l % 2 == 0:
            w["ffn"] = (ffn_w1[j].astype(MM), ffn_w3[j].astype(MM), ffn_w2[j].astype(MM))
        else:
            w["moe"] = (moe_router[j], moe_w1[j].astype(MM), moe_w3[j].astype(MM), moe_w2[j].astype(MM))
        lw.append(w)

    bp, tp, _ = x_prompt.shape
    bs, ts, _ = x_sample.shape
    zeros = lambda *s: jnp.zeros((depth, bp) + s, F32)
    y_p, (k_p, v_p, ki_p, bc_p, bs_p, cc_p, cs_p) = _trunk(
        x_prompt, p_prompt, lw, zeros(CONV_W - 1, CB), zeros(H_B, DK_B, DV_B), zeros(CONV_W - 1, CC),
        zeros(H_C, P_C, N_C), None, min(512, bp * tp), min(256, tp))
    y_s, (k_s, v_s, ki_s, bc_s, bs_s, cc_s, cs_s) = _trunk(
        x_sample, p_sample, lw, state_b_conv, state_b_ssm, state_c_conv, state_c_ssm,
        (cache_k, cache_v, cache_kidx, page_table), bs * ts, CHUNK)
    return (y_p, y_s, k_p, k_s, v_p, v_s, ki_p, ki_s, bc_p, bc_s, bs_p, bs_s, cc_p, cc_s, cs_p, cs_s)
```

```python
import functools
import math

import numpy as np
import jax
import jax.numpy as jnp
from jax import lax
from jax.experimental import pallas as pl
from jax.experimental.pallas import tpu as pltpu

F32 = jnp.float32
I32 = jnp.int32
MM = jnp.bfloat16
EPS = 1e-6
NEG = -1e30
INT_MIN = -2 ** 31
LOG2E = 1.4426950408889634

LANES = 128
SUBLANES = 8
VMEM_LIMIT = 56 * 1024 * 1024

H_A, DH_A, H_I, D_I, TOPK, Q_BLOCK = 8, 64, 8, 64, 256, 128
H_B, DK_B, DV_B = 4, 64, 64
H_C, P_C, G_C, N_C = 4, 64, 2, 128
CONV_W, CHUNK = 4, 64
N_EXPERTS, TOP_E = 8, 2
W_A, W_B, W_C = H_A * DH_A, H_B * DV_B, H_C * P_C
CB = 2 * H_B * DK_B + H_B * DV_B
CC = W_C + 2 * G_C * N_C

SM_KI, SM_WI, SM_BETA, SM_A, SM_DT = 0, 64, 72, 76, 80


def _dot(a, b):
    return jnp.dot(a, b, preferred_element_type=F32)


def _dot_nt(a, b):
    return lax.dot_general(a, b, (((1,), (1,)), ((), ())), preferred_element_type=F32)


def _dot_tn(a, b):
    return lax.dot_general(a, b, (((0,), (0,)), ((), ())), preferred_element_type=F32)


def _split2(x):
    hi = x.astype(MM)
    lo = (x - hi.astype(F32)).astype(MM)
    return hi, lo


def _split3(x):
    a1 = x.astype(MM)
    r = x - a1.astype(F32)
    a2 = r.astype(MM)
    a3 = (r - a2.astype(F32)).astype(MM)
    return a1, a2, a3


def _dot_hi(a, b, f=_dot):
    ah, al = _split2(a)
    bh, bl = _split2(b)
    return f(ah, bh) + (f(ah, bl) + f(al, bh))


def _dot_x3(a, b_exact):
    a1, a2, a3 = _split3(a)
    return _dot(a1, b_exact) + (_dot(a2, b_exact) + _dot(a3, b_exact))


def _x3_dot(a_exact, b):
    b1, b2, b3 = _split3(b)
    return _dot(a_exact, b1) + (_dot(a_exact, b2) + _dot(a_exact, b3))


def _group_mean(xx, bd):
    hi, lo = _split2(xx)
    return _dot(hi, bd) + _dot(lo, bd)


def _sigmoid(x):
    return 1.0 / (1.0 + jnp.exp(-x))


def _silu(x):
    return x * _sigmoid(x)


def _softplus(x):
    return jnp.maximum(x, 0.0) + jnp.log(1.0 + jnp.exp(-jnp.abs(x)))


def _rms(x, g):
    ms = jnp.mean(x * x, axis=-1, keepdims=True)
    return x * lax.rsqrt(ms + EPS) * g


def _block_diag(width, group, value):
    i = np.arange(width)
    return jnp.asarray(np.where((i[:, None] // group) == (i[None, :] // group), value, 0.0), MM)


def _params(sem):
    return pltpu.CompilerParams(dimension_semantics=sem, vmem_limit_bytes=VMEM_LIMIT)


def _full(shape):
    n = len(shape)
    return pl.BlockSpec(shape, lambda *_: (0,) * n)


def _inproj_kernel(h_ref, g_ref, w_ref, qn_ref, kn_ref, bd_ref,
                   q16_ref, k32_ref, k16_ref, v32_ref, v16_ref, qi16_ref,
                   sm32_ref, sm16_ref, bqkv_ref, bz_ref, cz_ref, cxbc_ref):
    u = _rms(h_ref[...], g_ref[...]).astype(MM)

    def proj(a, b):
        return _dot(u, w_ref[:, a:b])

    bd = bd_ref[...]
    aq = proj(0, W_A)
    q = aq * lax.rsqrt(_group_mean(aq * aq, bd) + EPS) * qn_ref[...]
    q16_ref[...] = (q * (DH_A ** -0.5 * LOG2E)).astype(MM)
    ak = proj(W_A, 2 * W_A)
    k = ak * lax.rsqrt(_group_mean(ak * ak, bd) + EPS) * kn_ref[...]
    k32_ref[...] = k
    k16_ref[...] = k.astype(MM)
    v = proj(2 * W_A, 3 * W_A)
    v32_ref[...] = v
    v16_ref[...] = v.astype(MM)
    qi16_ref[...] = proj(3 * W_A, 4 * W_A).astype(MM)
    o = 4 * W_A
    sm = proj(o, o + LANES)
    sm32_ref[...] = sm
    sm16_ref[...] = sm.astype(MM)
    o += LANES
    bqkv_ref[...] = proj(o, o + CB)
    bz_ref[...] = proj(o + CB, o + CB + W_B)
    o += CB + W_B
    cz_ref[...] = proj(o, o + W_C)
    cxbc_ref[...] = proj(o + W_C, o + W_C + CC)


def _arrange_w_in(w):
    d = w.shape[0]
    o_aik = 4 * W_A
    o_aiw = o_aik + D_I
    o_bqkv = o_aiw + H_I
    o_bz = o_bqkv + CB
    o_bbeta = o_bz + W_B
    o_ba = o_bbeta + H_B
    o_cz = o_ba + H_B
    o_cxbc = o_cz + W_C
    o_cdt = o_cxbc + CC
    small = jnp.concatenate(
        [w[:, o_aik:o_aiw], w[:, o_aiw:o_bqkv], w[:, o_bbeta:o_ba], w[:, o_ba:o_cz],
         w[:, o_cdt:o_cdt + H_C]], axis=1)
    small = jnp.pad(small, ((0, 0), (0, LANES - small.shape[1])))
    return jnp.concatenate(
        [w[:, :o_aik], small, w[:, o_bqkv:o_bbeta], w[:, o_cz:o_cdt]], axis=1).astype(MM)


def _inproj(h, g, w_arr, qn, kn, tm):
    n, d = h.shape
    wtot = w_arr.shape[1]
    bd = _block_diag(W_A, DH_A, 1.0 / DH_A)
    row = lambda w: pl.BlockSpec((tm, w), lambda i: (i, 0))
    outs = [(W_A, MM), (W_A, F32), (W_A, MM), (W_A, F32), (W_A, MM), (W_A, MM),
            (LANES, F32), (LANES, MM), (CB, F32), (W_B, F32), (W_C, F32), (CC, F32)]
    return pl.pallas_call(
        _inproj_kernel,
        grid=(n // tm,),
        in_specs=[row(d), _full((1, d)), _full((d, wtot)), _full((1, W_A)), _full((1, W_A)),
                  _full((W_A, W_A))],
        out_specs=[row(w) for w, _ in outs],
        out_shape=[jax.ShapeDtypeStruct((n, w), t) for w, t in outs],
        compiler_params=_params(("parallel",)),
        name="inproj",
    )(h, g.reshape(1, d), w_arr, jnp.tile(qn, H_A).reshape(1, W_A), jnp.tile(kn, H_A).reshape(1, W_A), bd)


def _mix_prologue(h_ref, ya_ref, yb_ref, yc_ref, wo_ref, gf_ref, h1_ref, u2_ref, acc_ref):
    h1 = h_ref[...] + (_dot(ya_ref[...], wo_ref[0:W_A, :])
                       + _dot(yb_ref[...], wo_ref[W_A:W_A + W_B, :])
                       + _dot(yc_ref[...], wo_ref[W_A + W_B:, :]))
    h1_ref[...] = h1
    u2 = _rms(h1, gf_ref[...])
    u2_ref[...] = u2.astype(MM)
    acc_ref[...] = jnp.zeros_like(acc_ref)
    return u2


def _mix_epilogue(p_ref, gp_ref, wg_ref, wp_ref, h1_ref, acc_ref, o_ref):
    h2 = h1_ref[...] + acc_ref[...]
    u3 = _rms(h2, gp_ref[...]).astype(MM)
    gate = _sigmoid(_dot(u3, wg_ref[...]))
    o_ref[...] = h2 + gate * _dot(p_ref[...].astype(MM), wp_ref[...])


def _ffn_kernel(h_ref, ya_ref, yb_ref, yc_ref, p_ref, wo_ref, gf_ref, w1_ref, w3_ref, w2_ref,
                gp_ref, wg_ref, wp_ref, o_ref, h1_ref, u2_ref, acc_ref):
    f = pl.program_id(1)

    @pl.when(f == 0)
    def _():
        _mix_prologue(h_ref, ya_ref, yb_ref, yc_ref, wo_ref, gf_ref, h1_ref, u2_ref, acc_ref)

    u2 = u2_ref[...]
    hid = _silu(_dot(u2, w1_ref[...])) * _dot(u2, w3_ref[...])
    acc_ref[...] += _dot(hid.astype(MM), w2_ref[...])

    @pl.when(f == pl.num_programs(1) - 1)
    def _():
        _mix_epilogue(p_ref, gp_ref, wg_ref, wp_ref, h1_ref, acc_ref, o_ref)


def _moe_kernel(h_ref, ya_ref, yb_ref, yc_ref, p_ref, wo_ref, gf_ref, wr_ref, w1_ref, w3_ref, w2_ref,
                gp_ref, wg_ref, wp_ref, o_ref, h1_ref, u2_ref, acc_ref, comb_ref):
    e = pl.program_id(1)
    f = pl.program_id(2)

    @pl.when((e == 0) & (f == 0))
    def _():
        u2 = _mix_prologue(h_ref, ya_ref, yb_ref, yc_ref, wo_ref, gf_ref, h1_ref, u2_ref, acc_ref)
        logits = _dot_hi(u2, wr_ref[...])
        lane = lax.broadcasted_iota(I32, logits.shape, 1)
        big = jnp.int32(LANES)
        lg = jnp.where(lane < N_EXPERTS, logits, -jnp.inf)
        m1 = jnp.max(lg, axis=1, keepdims=True)
        i1 = jnp.min(jnp.where(lg == m1, lane, big), axis=1, keepdims=True)
        lg2 = jnp.where(lane == i1, -jnp.inf, lg)
        m2 = jnp.max(lg2, axis=1, keepdims=True)
        i2 = jnp.min(jnp.where(lg2 == m2, lane, big), axis=1, keepdims=True)
        e2 = jnp.exp(m2 - m1)
        g1 = 1.0 / (1.0 + e2)
        g2 = e2 / (1.0 + e2)
        for ex in range(N_EXPERTS):
            ce = jnp.where(i1 == ex, g1, 0.0) + jnp.where(i2 == ex, g2, 0.0)
            comb_ref[ex] = jnp.broadcast_to(ce, comb_ref.shape[1:])

    u2 = u2_ref[...]
    hid = _silu(_dot(u2, w1_ref[0])) * _dot(u2, w3_ref[0])
    tf = hid.shape[1]
    ce = comb_ref[e]
    hid = hid * jnp.concatenate([ce] * (tf // LANES), axis=1)
    acc_ref[...] += _dot(hid.astype(MM), w2_ref[0])

    @pl.when((e == pl.num_programs(1) - 1) & (f == pl.num_programs(2) - 1))
    def _():
        _mix_epilogue(p_ref, gp_ref, wg_ref, wp_ref, h1_ref, acc_ref, o_ref)


def _channel_mix(h, ya, yb, yc, p, wo, gf, gp, wg, wp, tm, tf, *, ffn=None, moe=None):
    n, d = h.shape
    dple = p.shape[1]
    if ffn is not None:
        w1, w3, w2 = ffn
        dff = w1.shape[1]
        grid = (n // tm, dff // tf)
        row = lambda w: pl.BlockSpec((tm, w), lambda i, f: (i, 0))
        wspecs = [pl.BlockSpec((d, tf), lambda i, f: (0, f)), pl.BlockSpec((d, tf), lambda i, f: (0, f)),
                  pl.BlockSpec((tf, d), lambda i, f: (f, 0))]
        wargs = [w1, w3, w2]
        body = _ffn_kernel
        extra_scratch = []
        sem = ("parallel", "arbitrary")
    else:
        wr, w1, w3, w2 = moe
        ne, _, dff = w1.shape
        grid = (n // tm, ne, dff // tf)
        row = lambda w: pl.BlockSpec((tm, w), lambda i, e, f: (i, 0))
        wspecs = [_full((d, LANES)),
                  pl.BlockSpec((1, d, tf), lambda i, e, f: (e, 0, f)),
                  pl.BlockSpec((1, d, tf), lambda i, e, f: (e, 0, f)),
                  pl.BlockSpec((1, tf, d), lambda i, e, f: (e, f, 0))]
        wargs = [jnp.pad(wr, ((0, 0), (0, LANES - ne))), w1, w3, w2]
        body = _moe_kernel
        extra_scratch = [pltpu.VMEM((ne, tm, LANES), F32)]
        sem = ("parallel", "arbitrary", "arbitrary")
    return pl.pallas_call(
        body,
        grid=grid,
        in_specs=[row(d), row(W_A), row(W_B), row(W_C), row(dple), _full((d, d)), _full((1, d))]
                 + wspecs + [_full((1, d)), _full((d, d)), _full((dple, d))],
        out_specs=row(d),
        out_shape=jax.ShapeDtypeStruct((n, d), F32),
        scratch_shapes=[pltpu.VMEM((tm, d), F32), pltpu.VMEM((tm, d), MM), pltpu.VMEM((tm, d), F32)]
                       + extra_scratch,
        compiler_params=_params(sem),
        name="ffn" if ffn is not None else "moe",
    )(h, ya, yb, yc, p, wo, gf.reshape(1, d), *wargs, gp.reshape(1, d), wg, wp)


def _sortable(x):
    b = lax.bitcast_convert_type(x, I32)
    return b ^ ((b >> 31) & jnp.int32(0x7FFFFFFF))


def _select_topk(keys_ref, bias_ref, n_groups, group, topk, idx_bits, shape):
    zero = jnp.zeros(shape, I32)
    lane = lax.broadcasted_iota(I32, shape, 1)
    imin = jnp.int32(INT_MIN)

    def count(pred):
        def body(i, acc):
            for t in range(group):
                j = i * group + t
                acc = acc + jnp.where(pred(keys_ref[j], j), 1.0, 0.0)
            return acc
        acc = lax.fori_loop(0, n_groups, body, jnp.zeros(shape, F32))
        return jnp.broadcast_to(jnp.sum(acc, axis=1, keepdims=True), shape)

    def value_bit(b, carry):
        thr, cnt_thr = carry
        cand = thr + jnp.left_shift(jnp.int32(1), 31 - b)
        cnt = count(lambda k, j: k >= cand)
        keep = cnt >= topk
        return jnp.where(keep, cand, thr), jnp.where(keep, cnt, cnt_thr)

    thr, cnt_thr = lax.fori_loop(0, 32, value_bit, (jnp.full(shape, imin), jnp.full(shape, 2.0 ** 30, F32)))
    no_cut = jnp.full(shape, 2 ** idx_bits - 1, I32)
    tied = (cnt_thr > topk) & (thr > imin)

    def index_cut():
        need = topk - count(lambda k, j: k > thr)

        def index_bit(b, c):
            cc = c | jnp.left_shift(jnp.int32(1), idx_bits - 1 - b)
            cnt = count(lambda k, j: (k == thr) & (j * LANES + lane < cc))
            return jnp.where(cnt <= need, cc, c)

        return lax.fori_loop(0, idx_bits, index_bit, zero)

    cut = lax.cond(jnp.max(jnp.where(tied, 1.0, 0.0)) > 0.0, index_cut, lambda: no_cut)
    cut = jnp.where(tied, cut, no_cut)
    floor = jnp.maximum(thr, imin + 1)

    def write(i, carry):
        for t in range(group):
            j = i * group + t
            k = keys_ref[j]
            sel = (k > floor) | ((k == floor) & (j * LANES + lane < cut))
            bias_ref[j] = jnp.where(sel, 0.0, NEG)
        return carry

    lax.fori_loop(0, n_groups, write, 0)


def _dsa_prompt_kernel(q_ref, qi_ref, sm_ref, k_ref, v_ref, ki_ref, o_ref,
                       keys_ref, bias_ref, wb_ref, qp_ref, qip_ref, mx_ref, ls_ref, acc_ref,
                       *, topk, kc, idx_bits):
    i = pl.program_id(1)
    qb = q_ref.shape[0]
    tpc = kc // LANES
    n_chunks = ((i + 1) * qb + kc - 1) // kc
    n_tiles = n_chunks * tpc
    shape = (qb, LANES)
    lane = lax.broadcasted_iota(I32, shape, 1)
    pos = i * qb + lax.broadcasted_iota(I32, shape, 0)
    low = lane < DH_A

    wi = sm_ref[:, SM_WI:SM_WI + H_I] * ((D_I ** -0.5) * (H_I ** -0.5))
    for h in range(H_I):
        wb_ref[h] = jnp.broadcast_to(wi[:, h:h + 1], shape)
    for h in range(H_A):
        pr = h // 2
        blk = q_ref[:, pr * LANES:(pr + 1) * LANES]
        qp_ref[pr, (h % 2) * qb:(h % 2 + 1) * qb, :] = jnp.where(low if h % 2 == 0 else ~low, blk,
                                                                 jnp.zeros_like(blk))
        blk = qi_ref[:, pr * LANES:(pr + 1) * LANES]
        if h % 2 == 1:
            blk = pltpu.roll(blk, DH_A, 1)
        qip_ref[h * qb:(h + 1) * qb, :] = jnp.where(low, blk, jnp.zeros_like(blk))

    def score_chunk(c, carry):
        base = pl.multiple_of(c * kc, kc)
        kt = ki_ref[pl.ds(base, kc), :]
        s = _dot_nt(qip_ref[...], kt)
        for t in range(tpc):
            sc = jnp.zeros(shape, F32)
            for h in range(H_I):
                sc = sc + wb_ref[h] * jnp.maximum(s[h * qb:(h + 1) * qb, t * LANES:(t + 1) * LANES], 0.0)
            valid = base + t * LANES + lane <= pos
            keys_ref[c * tpc + t] = jnp.where(valid, _sortable(sc), jnp.int32(INT_MIN))
        return carry

    lax.fori_loop(0, n_chunks, score_chunk, 0)
    _select_topk(keys_ref, bias_ref, n_chunks, tpc, topk, idx_bits, shape)

    n_pairs = H_A // 2
    pshape = (2 * qb, LANES)
    for pr in range(n_pairs):
        mx_ref[pr] = jnp.full(pshape, NEG, F32)
        ls_ref[pr] = jnp.zeros(pshape, F32)
        acc_ref[pr] = jnp.zeros(pshape, F32)

    def qk(c, pr):
        base = pl.multiple_of(c * kc, kc)
        return _dot_nt(qp_ref[pr], k_ref[pl.ds(base, kc), pr * LANES:(pr + 1) * LANES])

    def masked(c, s):
        out = []
        for t in range(tpc):
            b = bias_ref[c * tpc + t]
            out.append(s[:, t * LANES:(t + 1) * LANES] + jnp.concatenate([b, b], axis=0))
        return out

    def max_chunk(c, carry):
        s = qk(c, 0)
        for pr in range(n_pairs):
            nxt = qk(c, pr + 1) if pr + 1 < n_pairs else None
            mx_ref[pr] = jnp.maximum(mx_ref[pr], functools.reduce(jnp.maximum, masked(c, s)))
            s = nxt
        return carry

    lax.fori_loop(0, n_chunks, max_chunk, 0)
    for pr in range(n_pairs):
        mx_ref[pr] = jnp.broadcast_to(jnp.max(mx_ref[pr], axis=1, keepdims=True), pshape)

    def pv_chunk(c, carry):
        base = pl.multiple_of(c * kc, kc)
        s = qk(c, 0)
        for pr in range(n_pairs):
            nxt = qk(c, pr + 1) if pr + 1 < n_pairs else None
            m = mx_ref[pr]
            ps = [jnp.exp2(x - m) for x in masked(c, s)]
            ls_ref[pr] += functools.reduce(jnp.add, ps)
            p = jnp.concatenate(ps, axis=1).astype(MM)
            acc_ref[pr] += _dot(p, v_ref[pl.ds(base, kc), pr * LANES:(pr + 1) * LANES])
            s = nxt
        return carry

    lax.fori_loop(0, n_chunks, pv_chunk, 0)
    for pr in range(n_pairs):
        o = acc_ref[pr] / jnp.sum(ls_ref[pr], axis=1, keepdims=True)
        o_ref[:, pr * LANES:(pr + 1) * LANES] = jnp.where(low, o[:qb], o[qb:]).astype(o_ref.dtype)


def _dsa_prompt(q16, qi16, sm32, k16, v16, sm16, batch, seq):
    qb = Q_BLOCK
    topk = min(TOPK, seq // 4)
    kc = math.gcd(seq, 512)
    nq = seq // qb
    idx_bits = int(seq).bit_length() + 1
    blk = lambda w: pl.BlockSpec((qb, w), lambda b, i: (b * nq + i, 0))
    seqblk = lambda w: pl.BlockSpec((seq, w), lambda b, i: (b, 0), pipeline_mode=pl.Buffered(1))
    tiles = (seq // LANES, qb, LANES)
    return pl.pallas_call(
        functools.partial(_dsa_prompt_kernel, topk=topk, kc=kc, idx_bits=idx_bits),
        grid=(batch, nq),
        in_specs=[blk(W_A), blk(W_A), blk(LANES), seqblk(W_A), seqblk(W_A), seqblk(LANES)],
        out_specs=blk(W_A),
        out_shape=jax.ShapeDtypeStruct((batch * seq, W_A), MM),
        scratch_shapes=[pltpu.VMEM(tiles, I32), pltpu.VMEM(tiles, F32),
                        pltpu.VMEM((H_I, qb, LANES), F32), pltpu.VMEM((H_A // 2, 2 * qb, LANES), MM),
                        pltpu.VMEM((H_I * qb, LANES), MM), pltpu.VMEM((H_A // 2, 2 * qb, LANES), F32),
                        pltpu.VMEM((H_A // 2, 2 * qb, LANES), F32), pltpu.VMEM((H_A // 2, 2 * qb, LANES), F32)],
        compiler_params=_params(("parallel", "arbitrary")),
        name="dsa_prompt",
    )(q16, qi16, sm32, k16, v16, sm16)


def _expand_matrix(first_lane, heads, width):
    m = np.zeros((LANES, heads * width), np.float32)
    for h in range(heads):
        m[first_lane + h, h * width:(h + 1) * width] = 1.0
    return jnp.asarray(m, MM)


def _chunk_masks(tt, chunk):
    i = np.arange(tt)
    same = (i[:, None] // chunk) == (i[None, :] // chunk)
    ltri = jnp.asarray(np.where(same & (i[:, None] >= i[None, :]), 1.0, 0.0), MM)
    utri = jnp.asarray(np.where(same & (i[:, None] <= i[None, :]), 1.0, 0.0), F32)
    return ltri, utri


def _lane_vec(values, first_lane):
    return jnp.zeros((1, LANES), F32).at[0, first_lane:first_lane + values.shape[0]].set(values)


def _conv_tile(x_ref, buf_ref, cw_ref, xe_ref, nb_ref, t, n_t, tv_last):
    tt = x_ref.shape[0]
    hist = SUBLANES

    @pl.when(t == 0)
    def _():
        xe_ref[0:hist, :] = jnp.zeros((hist, xe_ref.shape[1]), F32)
        xe_ref[hist - (CONV_W - 1):hist, :] = buf_ref[0]

    xe_ref[hist:hist + tt, :] = x_ref[...]
    y = jnp.zeros(x_ref.shape, F32)
    for i in range(CONV_W):
        o = hist - (CONV_W - 1) + i
        y = y + xe_ref[o:o + tt, :] * cw_ref[i:i + 1, :]

    @pl.when(t == n_t - 1)
    def _():
        nb_ref[0] = xe_ref[hist + tv_last - (CONV_W - 1):hist + tv_last, :]

    xe_ref[0:hist, :] = xe_ref[tt:tt + hist, :]
    return y


def _live_rows(shape, t, n_t, tv_last):
    row = lax.broadcasted_iota(I32, shape, 0)
    return (t < n_t - 1) | (row < tv_last)


def _row_cumsum(vals, utri, ones_c, heads, width, chunk):
    tt = vals.shape[0]
    out = []
    for h in range(heads):
        col = vals[:, h * width:h * width + chunk]
        out.append(_x3_dot(ones_c, jnp.concatenate([col] * (tt // chunk), axis=1) * utri))
    return out


def _gdn_kernel(x_ref, z_ref, sm_ref, buf_ref, s0_ref, cw_ref, alog_ref, dtb_ref, ng_ref,
                ebeta_ref, eg_ref, ltri_ref, utri_ref, bd1_ref, bdn_ref,
                y_ref, nb_ref, st_ref, xe_ref, s_ref, o_ref, *, tv_last, chunk):
    t = pl.program_id(1)
    n_t = pl.num_programs(1)
    tt = x_ref.shape[0]
    wq = H_B * DK_B

    @pl.when(t == 0)
    def _():
        s_ref[...] = s0_ref[0]

    qkv = _silu(_conv_tile(x_ref, buf_ref, cw_ref, xe_ref, nb_ref, t, n_t, tv_last))
    qk = qkv[:, :2 * wq]
    qk = qk * lax.rsqrt(_group_mean(qk * qk, bd1_ref[...]) + EPS)
    q = qk[:, :wq] * (DK_B ** -0.5)
    k = qk[:, wq:]
    v = qkv[:, 2 * wq:]

    sm = sm_ref[...]
    beta = _sigmoid(sm)
    g = -jnp.exp(alog_ref[...]) * _softplus(sm + dtb_ref[...])
    if tv_last < tt:
        live = _live_rows(sm.shape, t, n_t, tv_last)
        beta = jnp.where(live, beta, 0.0)
        g = jnp.where(live, g, 0.0)
    betab = _dot_x3(beta, ebeta_ref[...])
    gb = _dot_x3(g, eg_ref[...])
    gc = _x3_dot(ltri_ref[...], gb)
    ones_c = jnp.ones((chunk, tt), MM)
    gr = _row_cumsum(gb, utri_ref[...], ones_c, H_B, DK_B, chunk)
    eg = jnp.exp(gc)
    kb = k * betab
    vb = v * betab
    kbeg = kb * eg
    qg = q * eg
    ii = lax.broadcasted_iota(I32, (chunk, chunk), 0)
    jj = lax.broadcasted_iota(I32, (chunk, chunk), 1)
    levels = max(1, (chunk - 1).bit_length())

    n_c = tt // chunk
    units = [(c, h) for c in range(n_c) for h in range(H_B)]
    rows = lambda c: slice(c * chunk, (c + 1) * chunk)
    cols = lambda h: slice(h * DK_B, (h + 1) * DK_B)
    k16 = k.astype(MM)
    kk = {u: _dot_nt(kb[rows(u[0]), cols(u[1])].astype(MM), k16[rows(u[0]), cols(u[1])]) for u in units}
    qk = {u: _dot_nt(q[rows(u[0]), cols(u[1])].astype(MM), k16[rows(u[0]), cols(u[1])]) for u in units}
    xs, ps, attn = {}, {}, {}
    for u in units:
        c, h = u
        decay = jnp.exp(jnp.where(ii >= jj, gc[rows(c), cols(h)] - gr[h][:, rows(c)], NEG))
        ps[u] = -jnp.where(ii > jj, kk[u] * decay, 0.0)
        attn[u] = (qk[u] * decay).astype(MM)
        xs[u] = jnp.concatenate([vb[rows(c), cols(h)], kbeg[rows(c), cols(h)]], axis=1)
    wx = 2 * DV_B
    for lv in range(levels):
        last_lv = lv == levels - 1
        prod = {u: _dot_hi(ps[u], xs[u] if last_lv else jnp.concatenate([xs[u], ps[u]], axis=1)) for u in units}
        for u in units:
            xs[u] = xs[u] + prod[u][:, :wx]
            if not last_lv:
                ps[u] = prod[u][:, wx:]

    for c in range(n_c):
        last = gc[(c + 1) * chunk - 1:(c + 1) * chunk, :]
        kg = (k[rows(c), :] * jnp.exp(last - gc[rows(c), :])).astype(MM)
        gl = jnp.exp(last)
        qg16 = qg[rows(c), :].astype(MM)
        s_old = [s_ref[h] for h in range(H_B)]
        s16 = [s.astype(MM) for s in s_old]
        ws = [_dot(xs[c, h][:, DV_B:].astype(MM), s16[h]) for h in range(H_B)]
        qs = [_dot(qg16[:, cols(h)], s16[h]) for h in range(H_B)]
        u16 = [(xs[c, h][:, :DV_B] - ws[h]).astype(MM) for h in range(H_B)]
        au = [_dot(attn[c, h], u16[h]) for h in range(H_B)]
        ku = [_dot_tn(kg[:, cols(h)], u16[h]) for h in range(H_B)]
        for h in range(H_B):
            o_ref[rows(c), cols(h)] = qs[h] + au[h]
            s_ref[h] = s_old[h] * gl[:, h * DK_B:h * DK_B + 1] + ku[h]

    o = o_ref[...]
    o = o * lax.rsqrt(_group_mean(o * o, bdn_ref[...]) + EPS) * ng_ref[...]
    y_ref[...] = (o * _silu(z_ref[...])).astype(y_ref.dtype)

    @pl.when(t == n_t - 1)
    def _():
        st_ref[0] = s_ref[...]


def _ssd_kernel(x_ref, z_ref, sm_ref, buf_ref, h0_ref, cw_ref, cb_ref, alog_ref, dtb_ref, d_ref, ng_ref,
                edt_ref, ltri_ref, utri_ref, bdn_ref,
                y_ref, nb_ref, ht_ref, xe_ref, h_ref, o_ref, *, tv_last, chunk):
    t = pl.program_id(1)
    n_t = pl.num_programs(1)
    tt = x_ref.shape[0]
    gw = G_C * N_C
    hpg = H_C // G_C

    @pl.when(t == 0)
    def _():
        h_ref[...] = h0_ref[0]

    xbc = _silu(_conv_tile(x_ref, buf_ref, cw_ref, xe_ref, nb_ref, t, n_t, tv_last) + cb_ref[...])
    xs = xbc[:, :W_C]
    bm = xbc[:, W_C:W_C + gw].astype(MM)
    cm = xbc[:, W_C + gw:].astype(MM)

    dt = _softplus(sm_ref[...] + dtb_ref[...])
    if tv_last < tt:
        dt = jnp.where(_live_rows(dt.shape, t, n_t, tv_last), dt, 0.0)
    dtb = _dot_x3(dt, edt_ref[...])
    a = dtb * (-jnp.exp(alog_ref[...]))
    xdt = xs * dtb
    ac = _x3_dot(ltri_ref[...], a)
    ones_c = jnp.ones((chunk, tt), MM)
    ar = _row_cumsum(a, utri_ref[...], ones_c, H_C, P_C, chunk)
    ea = jnp.exp(ac)
    ii = lax.broadcasted_iota(I32, (chunk, chunk), 0)
    jj = lax.broadcasted_iota(I32, (chunk, chunk), 1)

    for c in range(tt // chunk):
        rs = slice(c * chunk, (c + 1) * chunk)
        last = ac[(c + 1) * chunk - 1:(c + 1) * chunk, :]
        xdec = (xdt[rs, :] * jnp.exp(last - ac[rs, :])).astype(MM)
        hdec = jnp.exp(last)
        for g in range(G_C):
            gs = slice(g * N_C, (g + 1) * N_C)
            cbm = _dot_nt(cm[rs, gs], bm[rs, gs])
            for h in range(g * hpg, (g + 1) * hpg):
                hs = slice(h * P_C, (h + 1) * P_C)
                lmat = jnp.exp(jnp.where(ii >= jj, ac[rs, hs] - ar[h][:, rs], NEG))
                hprev = h_ref[h]
                y = _dot((cbm * lmat).astype(MM), xdt[rs, hs].astype(MM))
                y = y + ea[rs, hs] * _dot_nt(cm[rs, gs], hprev.astype(MM))
                o_ref[rs, hs] = y + d_ref[:, hs] * xs[rs, hs]
                h_ref[h] = hprev * hdec[:, h * P_C:h * P_C + 1] + _dot_tn(xdec[:, hs], bm[rs, gs])

    yz = o_ref[...] * _silu(z_ref[...])
    y_ref[...] = (yz * lax.rsqrt(_group_mean(yz * yz, bdn_ref[...]) + EPS) * ng_ref[...]).astype(y_ref.dtype)

    @pl.when(t == n_t - 1)
    def _():
        ht_ref[0] = h_ref[...]


def _recurrent_call(body, name, x, z, sm, buf, state0, vecs, consts, batch, t_pad, tt, tv_last, width_out):
    n_t = t_pad // tt
    cw = x.shape[1]
    blk = lambda w: pl.BlockSpec((tt, w), lambda b, t: (b * n_t + t, 0))
    per_b = lambda shp: pl.BlockSpec((1,) + shp, lambda b, t: (b,) + (0,) * len(shp))
    sshape = state0.shape[1:]
    return pl.pallas_call(
        functools.partial(body, tv_last=tv_last, chunk=min(CHUNK, tt)),
        grid=(batch, n_t),
        in_specs=[blk(cw), blk(width_out), blk(LANES), per_b((CONV_W - 1, cw)), per_b(sshape)]
                 + [_full(v.shape) for v in vecs] + [_full(c.shape) for c in consts],
        out_specs=[blk(width_out), per_b((CONV_W - 1, cw)), per_b(sshape)],
        out_shape=[jax.ShapeDtypeStruct((batch * t_pad, width_out), MM),
                   jax.ShapeDtypeStruct((batch, CONV_W - 1, cw), F32),
                   jax.ShapeDtypeStruct((batch,) + sshape, F32)],
        scratch_shapes=[pltpu.VMEM((tt + SUBLANES, cw), F32), pltpu.VMEM(sshape, F32),
                        pltpu.VMEM((tt, width_out), F32)],
        compiler_params=_params(("parallel", "arbitrary")),
        name=name,
    )(x, z, sm, buf, state0, *vecs, *consts)


def _gdn(x, z, sm, buf, s0, conv_w, a_log, dt_bias, norm_g, batch, t_pad, tt, tv_last):
    ltri, utri = _chunk_masks(tt, min(CHUNK, tt))
    vecs = [conv_w, _lane_vec(a_log, SM_A), _lane_vec(dt_bias, SM_A), jnp.tile(norm_g, H_B).reshape(1, W_B)]
    consts = [_expand_matrix(SM_BETA, H_B, DK_B), _expand_matrix(SM_A, H_B, DK_B), ltri, utri,
              _block_diag(2 * H_B * DK_B, DK_B, 1.0), _block_diag(W_B, DV_B, 1.0 / DV_B)]
    return _recurrent_call(_gdn_kernel, "gdn", x, z, sm, buf, s0, vecs, consts, batch, t_pad, tt, tv_last, W_B)


def _ssd(x, z, sm, buf, h0, conv_w, conv_b, a_log, dt_bias, dskip, norm_g, batch, t_pad, tt, tv_last):
    ltri, utri = _chunk_masks(tt, min(CHUNK, tt))
    vecs = [conv_w, conv_b.reshape(1, CC), jnp.repeat(a_log, P_C).reshape(1, W_C), _lane_vec(dt_bias, SM_DT),
            jnp.repeat(dskip, P_C).reshape(1, W_C), norm_g.reshape(1, W_C)]
    consts = [_expand_matrix(SM_DT, H_C, P_C), ltri, utri, _block_diag(W_C, W_C // G_C, float(G_C) / W_C)]
    return _recurrent_call(_ssd_kernel, "ssd", x, z, sm, buf, h0, vecs, consts, batch, t_pad, tt, tv_last, W_C)


def _dsa_sample_select_kernel(pt_ref, qi_ref, w_ref, kin_ref, *rest, topk, n_q, idx_bits, pg, group):
    pages = rest[:pg]
    bias_ref, keys_ref = rest[pg:]
    j = pl.program_id(1)
    n_s = pl.num_programs(1)
    n_tiles = bias_ref.shape[1]
    shape = (SUBLANES, LANES)
    row = lax.broadcasted_iota(I32, shape, 0)
    lane = lax.broadcasted_iota(I32, shape, 1)
    imin = jnp.int32(INT_MIN)

    def keys_of(s, valid):
        r = jnp.maximum(s, 0.0) * w_ref[0]
        per_q = [jnp.sum(r[q * H_I:(q + 1) * H_I], axis=0, keepdims=True) for q in range(n_q)]
        sc = jnp.concatenate(per_q + [jnp.zeros((SUBLANES - n_q, LANES), F32)], axis=0)
        return jnp.where(valid, _sortable(sc), imin)

    qi = qi_ref[0]
    ss = [_dot(qi, pages[t][0, 0].astype(MM)) for t in range(pg)]
    for t in range(pg):
        keys_ref[j * pg + t] = keys_of(ss[t], row < n_q)

    @pl.when(j == n_s - 1)
    def _():
        keys_ref[n_tiles - 1] = keys_of(_dot(qi, kin_ref[0]), (row < n_q) & (lane <= row))
        _select_topk(keys_ref, bias_ref.at[0], n_tiles // group, group, topk, idx_bits, shape)


def _dsa_sample_attend_kernel(pt_ref, q_ref, bias_ref, bnew_ref, kn_ref, vn_ref, *rest, pg):
    kps, vps = rest[:pg], rest[pg:2 * pg]
    o_ref, qbd_ref, m_ref, l_ref, acc_ref = rest[2 * pg:]
    j = pl.program_id(1)
    n_s = pl.num_programs(1)
    rows = H_A * SUBLANES
    shape = (rows, LANES)
    head_of_row = lax.broadcasted_iota(I32, (rows, W_A), 0) // SUBLANES
    head_of_lane = lax.broadcasted_iota(I32, (rows, W_A), 1) // DH_A
    own = head_of_row == head_of_lane

    @pl.when(j == 0)
    def _():
        qrep = jnp.concatenate([q_ref[0]] * H_A, axis=0)
        qbd_ref[...] = jnp.where(own, qrep, jnp.zeros_like(qrep))
        m_ref[...] = jnp.full(shape, NEG, F32)
        l_ref[...] = jnp.zeros(shape, F32)
        acc_ref[...] = jnp.zeros((rows, W_A), F32)

    def absorb(kks, vvs, biases):
        qbd = qbd_ref[...]
        ss = [_dot(qbd, kk) for kk in kks]
        bs = [jnp.concatenate([b] * H_A, axis=0) for b in biases]
        ss = [s + b for s, b in zip(ss, bs)]
        m_old = m_ref[...]
        mx = jnp.max(functools.reduce(jnp.maximum, ss), axis=1, keepdims=True)
        m_new = jnp.maximum(m_old, jnp.broadcast_to(mx, shape))
        alpha = jnp.exp2(m_old - m_new)
        ps = [jnp.where(b == 0.0, jnp.exp2(s - m_new), 0.0) for s, b in zip(ss, bs)]
        psum = jnp.sum(functools.reduce(jnp.add, ps), axis=1, keepdims=True)
        l_ref[...] = l_ref[...] * alpha + jnp.broadcast_to(psum, shape)
        pv = functools.reduce(jnp.add, [_dot_nt(p.astype(MM), vv) for p, vv in zip(ps, vvs)])
        acc_ref[...] = acc_ref[...] * jnp.concatenate([alpha] * (W_A // LANES), axis=1) + pv
        m_ref[...] = m_new

    absorb([kp[0, 0].astype(MM) for kp in kps], [vp[0, 0].astype(MM) for vp in vps],
           [bias_ref[0, t] for t in range(pg)])

    @pl.when(j == n_s - 1)
    def _():
        absorb([kn_ref[0]], [vn_ref[0]], [bnew_ref[0, 0]])
        o = acc_ref[...] / jnp.concatenate([l_ref[...]] * (W_A // LANES), axis=1)
        o = jnp.where(own, o, 0.0)
        out = functools.reduce(jnp.add, [o[h * SUBLANES:(h + 1) * SUBLANES] for h in range(H_A)])
        o_ref[0] = out.astype(o_ref.dtype)


def _dsa_sample(layer, q16, qi16, sm32, sm16, k16, v16, cache_k, cache_v, cache_ki, page_table, bs, ts):
    n_pages = page_table.shape[1]
    page = cache_ki.shape[2]
    assert page == LANES and ts <= SUBLANES
    past = n_pages * page
    topk = min(TOPK, (past + ts) // 4)
    idx_bits = int(past + ts).bit_length() + 1
    pt = page_table.reshape(-1)
    rq = ts * H_I
    qi_r = qi16.reshape(bs, rq, D_I)
    wi = sm32[:, SM_WI:SM_WI + H_I] * ((D_I ** -0.5) * (H_I ** -0.5))
    w_r = jnp.broadcast_to(wi.reshape(bs, rq, 1), (bs, rq, LANES))
    pad_rows = lambda a, rows: jnp.pad(a.reshape(bs, ts, -1), ((0, 0), (0, rows - ts), (0, 0)))
    new_t = lambda a: jnp.pad(jnp.swapaxes(a.reshape(bs, ts, -1), 1, 2), ((0, 0), (0, 0), (0, page - ts)))
    kin = new_t(sm16[:, SM_KI:SM_KI + D_I])
    per_b = lambda shp: pl.BlockSpec((1,) + shp, lambda b, j, p: (b,) + (0,) * len(shp))
    pg = math.gcd(n_pages, 8)
    n_steps = n_pages // pg
    paged = lambda w, t: pl.BlockSpec(
        (1, 1, w, page), lambda b, j, p: (layer, p[b * n_pages + j * pg + t], 0, 0))
    tiles = n_pages + 1
    group = max(g for g in range(1, 9) if tiles % g == 0)

    bias = pl.pallas_call(
        functools.partial(_dsa_sample_select_kernel, topk=topk, n_q=ts, idx_bits=idx_bits, pg=pg, group=group),
        grid_spec=pltpu.PrefetchScalarGridSpec(
            num_scalar_prefetch=1, grid=(bs, n_steps),
            in_specs=[per_b((rq, D_I)), per_b((rq, LANES)), per_b((D_I, page))]
                     + [paged(D_I, t) for t in range(pg)],
            out_specs=per_b((tiles, SUBLANES, LANES)),
            scratch_shapes=[pltpu.VMEM((tiles, SUBLANES, LANES), I32)]),
        out_shape=jax.ShapeDtypeStruct((bs, tiles, SUBLANES, LANES), F32),
        compiler_params=_params(("parallel", "arbitrary")),
        name="dsa_sample_select",
    )(pt, qi_r, w_r, kin, *([jnp.swapaxes(cache_ki, 2, 3)] * pg))

    page_t = lambda c: jnp.transpose(c, (0, 1, 3, 4, 2)).reshape(c.shape[:2] + (W_A, page))
    ck = page_t(cache_k)
    cv = page_t(cache_v)
    rows = H_A * SUBLANES
    out = pl.pallas_call(
        functools.partial(_dsa_sample_attend_kernel, pg=pg),
        grid_spec=pltpu.PrefetchScalarGridSpec(
            num_scalar_prefetch=1, grid=(bs, n_steps),
            in_specs=[per_b((SUBLANES, W_A)),
                      pl.BlockSpec((1, pg, SUBLANES, LANES), lambda b, j, p: (b, j, 0, 0)),
                      pl.BlockSpec((1, 1, SUBLANES, LANES), lambda b, j, p: (b, n_pages, 0, 0)),
                      per_b((W_A, page)), per_b((W_A, page))]
                     + [paged(W_A, t) for t in range(pg)] * 2,
            out_specs=per_b((SUBLANES, W_A)),
            scratch_shapes=[pltpu.VMEM((rows, W_A), MM), pltpu.VMEM((rows, LANES), F32),
                            pltpu.VMEM((rows, LANES), F32), pltpu.VMEM((rows, W_A), F32)]),
        out_shape=jax.ShapeDtypeStruct((bs, SUBLANES, W_A), MM),
        compiler_params=_params(("parallel", "arbitrary")),
        name="dsa_sample_attend",
    )(pt, pad_rows(q16, SUBLANES), bias, bias, new_t(k16), new_t(v16), *([ck] * pg), *([cv] * pg))
    return out[:, :ts].reshape(bs * ts, W_A)


def _pad_seq(a, batch, t, t_pad):
    if t == t_pad:
        return a
    return jnp.pad(a.reshape(batch, t, -1), ((0, 0), (0, t_pad - t), (0, 0))).reshape(batch * t_pad, -1)


def _trunk(x, p, lw, conv_b0, ssm_b0, conv_c0, ssm_c0, kv, tm, tt):
    batch, t, d = x.shape
    n = batch * t
    depth = len(lw)
    t_pad = -(-t // tt) * tt
    tv_last = t - (t_pad - tt)
    assert min(CHUNK, tt) == DK_B == P_C and tv_last >= CONV_W - 1
    h = x.reshape(n, d)
    rec = [[] for _ in range(7)]
    for l, w in enumerate(lw):
        (q16, k32, k16, v32, v16, qi16, sm32, sm16, bqkv, bz, cz, cxbc) = _inproj(
            h, w["g_mix"], w["w_in"], w["q_norm"], w["k_norm"], tm)
        if kv is None:
            ya = _dsa_prompt(q16, qi16, sm32, k16, v16, sm16, batch, t)
        else:
            ya = _dsa_sample(l, q16, qi16, sm32, sm16, k16, v16, kv[0], kv[1], kv[2], kv[3], batch, t)
        pad = lambda a: _pad_seq(a, batch, t, t_pad)
        unpad = lambda a: a if t == t_pad else a.reshape(batch, t_pad, -1)[:, :t].reshape(n, -1)
        smp = pad(sm32)
        yb, bconv, bs_ = _gdn(pad(bqkv), pad(bz), smp, conv_b0[l], ssm_b0[l], w["b_conv_w"], w["b_A_log"],
                              w["b_dt_bias"], w["b_norm"], batch, t_pad, tt, tv_last)
        yc, cconv, cs_ = _ssd(pad(cxbc), pad(cz), smp, conv_c0[l], ssm_c0[l], w["c_conv_w"], w["c_conv_b"],
                              w["c_A_log"], w["c_dt_bias"], w["c_D"], w["c_norm"], batch, t_pad, tt, tv_last)
        h = _channel_mix(h, ya, unpad(yb), unpad(yc), p[l].reshape(n, -1), w["w_out"], w["g_ffn"], w["g_ple"],
                         w["w_ple_gate"], w["w_ple_proj"], tm, 512, ffn=w.get("ffn"), moe=w.get("moe"))
        for r, a in zip(rec, (k32.reshape(batch, t, H_A, DH_A), v32.reshape(batch, t, H_A, DH_A),
                              sm32[:, SM_KI:SM_KI + D_I].reshape(batch, t, D_I), bconv, bs_, cconv, cs_)):
            r.append(a)
    return h.reshape(batch, t, d), [jnp.stack(r) for r in rec]


def kernel(x_prompt, x_sample, cache_k, cache_v, cache_kidx, state_b_conv, state_b_ssm, state_c_conv, state_c_ssm, page_table, p_prompt, p_sample, w_in, w_out, g_mix, g_ffn, g_ple, q_norm, k_norm, b_conv_w, b_A_log, b_dt_bias, b_norm, c_conv_w, c_conv_b, c_A_log, c_dt_bias, c_D, c_norm, ffn_w1, ffn_w3, ffn_w2, moe_router, moe_w1, moe_w3, moe_w2, w_ple_gate, w_ple_proj):
    depth = w_in.shape[0]
    lw = []
    for l in range(depth):
        w = dict(w_in=_arrange_w_in(w_in[l]), w_out=w_out[l].astype(MM), g_mix=g_mix[l], g_ffn=g_ffn[l],
                 g_ple=g_ple[l], q_norm=q_norm[l], k_norm=k_norm[l], b_conv_w=b_conv_w[l], b_A_log=b_A_log[l],
                 b_dt_bias=b_dt_bias[l], b_norm=b_norm[l], c_conv_w=c_conv_w[l], c_conv_b=c_conv_b[l],
                 c_A_log=c_A_log[l], c_dt_bias=c_dt_bias[l], c_D=c_D[l], c_norm=c_norm[l],
                 w_ple_gate=w_ple_gate[l].astype(MM), w_ple_proj=w_ple_proj[l].astype(MM))
        j = l // 2
        if l % 2 == 0:
            w["ffn"] = (ffn_w1[j].astype(MM), ffn_w3[j].astype(MM), ffn_w2[j].astype(MM))
        else:
            w["moe"] = (moe_router[j], moe_w1[j].astype(MM), moe_w3[j].astype(MM), moe_w2[j].astype(MM))
        lw.append(w)

    bp, tp, _ = x_prompt.shape
    bs, ts, _ = x_sample.shape
    zeros = lambda *s: jnp.zeros((depth, bp) + s, F32)
    y_p, (k_p, v_p, ki_p, bc_p, bs_p, cc_p, cs_p) = _trunk(
        x_prompt, p_prompt, lw, zeros(CONV_W - 1, CB), zeros(H_B, DK_B, DV_B), zeros(CONV_W - 1, CC),
        zeros(H_C, P_C, N_C), None, min(512, bp * tp), min(256, tp))
    y_s, (k_s, v_s, ki_s, bc_s, bs_s, cc_s, cs_s) = _trunk(
        x_sample, p_sample, lw, state_b_conv, state_b_ssm, state_c_conv, state_c_ssm,
        (cache_k, cache_v, cache_kidx, page_table), bs * ts, CHUNK)
    return (y_p, y_s, k_p, k_s, v_p, v_s, ki_p, ki_s, bc_p, bc_s, bs_p, bs_s, cc_p, cc_s, cs_p, cs_s)
```

```python
import functools
import math

import numpy as np
import jax
import jax.numpy as jnp
from jax import lax
from jax.experimental import pallas as pl
from jax.experimental.pallas import tpu as pltpu

F32 = jnp.float32
I32 = jnp.int32
MM = jnp.bfloat16
EPS = 1e-6
NEG = -1e30
INT_MIN = -2 ** 31
LOG2E = 1.4426950408889634

LANES = 128
SUBLANES = 8
VMEM_LIMIT = 56 * 1024 * 1024

H_A, DH_A, H_I, D_I, TOPK, Q_BLOCK = 8, 64, 8, 64, 256, 128
H_B, DK_B, DV_B = 4, 64, 64
H_C, P_C, G_C, N_C = 4, 64, 2, 128
CONV_W, CHUNK = 4, 64
N_EXPERTS, TOP_E = 8, 2
FF_TILE = 896
W_A, W_B, W_C = H_A * DH_A, H_B * DV_B, H_C * P_C
CB = 2 * H_B * DK_B + H_B * DV_B
CC = W_C + 2 * G_C * N_C

SM_KI, SM_WI, SM_BETA, SM_A, SM_DT = 0, 64, 72, 76, 80


def _dot(a, b):
    return jnp.dot(a, b, preferred_element_type=F32)


def _dot_nt(a, b):
    return lax.dot_general(a, b, (((1,), (1,)), ((), ())), preferred_element_type=F32)


def _dot_tn(a, b):
    return lax.dot_general(a, b, (((0,), (0,)), ((), ())), preferred_element_type=F32)


def _split2(x):
    hi = x.astype(MM)
    lo = (x - hi.astype(F32)).astype(MM)
    return hi, lo


def _split3(x):
    a1 = x.astype(MM)
    r = x - a1.astype(F32)
    a2 = r.astype(MM)
    a3 = (r - a2.astype(F32)).astype(MM)
    return a1, a2, a3


def _dot_hi(a, b, f=_dot):
    ah, al = _split2(a)
    bh, bl = _split2(b)
    return f(ah, bh) + (f(ah, bl) + f(al, bh))


def _dot_x3(a, b_exact):
    a1, a2, a3 = _split3(a)
    return _dot(a1, b_exact) + (_dot(a2, b_exact) + _dot(a3, b_exact))


def _x3_dot(a_exact, b):
    b1, b2, b3 = _split3(b)
    return _dot(a_exact, b1) + (_dot(a_exact, b2) + _dot(a_exact, b3))


def _group_mean(xx, bd):
    hi, lo = _split2(xx)
    return _dot(hi, bd) + _dot(lo, bd)


def _sigmoid(x):
    return 1.0 / (1.0 + jnp.exp(-x))


def _silu(x):
    return x * _sigmoid(x)


def _softplus(x):
    return jnp.maximum(x, 0.0) + jnp.log(1.0 + jnp.exp(-jnp.abs(x)))


def _rms(x, g):
    ms = jnp.mean(x * x, axis=-1, keepdims=True)
    return x * lax.rsqrt(ms + EPS) * g


def _block_diag(width, group, value):
    i = np.arange(width)
    return jnp.asarray(np.where((i[:, None] // group) == (i[None, :] // group), value, 0.0), MM)


def _params(sem):
    return pltpu.CompilerParams(dimension_semantics=sem, vmem_limit_bytes=VMEM_LIMIT)


def _full(shape):
    n = len(shape)
    return pl.BlockSpec(shape, lambda *_: (0,) * n)


def _inproj_kernel(h_ref, g_ref, w_ref, qn_ref, kn_ref, bd_ref,
                   q16_ref, k32_ref, k16_ref, v32_ref, v16_ref, qi16_ref,
                   sm32_ref, sm16_ref, bqkv_ref, bz_ref, cz_ref, cxbc_ref, *smt_ref, feature_major):
    u = _rms(h_ref[...], g_ref[...]).astype(MM)

    def proj(a, b):
        return _dot_nt(u, w_ref[a:b, :])

    bd = bd_ref[...]
    aq = proj(0, W_A)
    q = aq * lax.rsqrt(_group_mean(aq * aq, bd) + EPS) * qn_ref[...]
    q16_ref[...] = (q * (DH_A ** -0.5 * LOG2E)).astype(MM)
    ak = proj(W_A, 2 * W_A)
    k = ak * lax.rsqrt(_group_mean(ak * ak, bd) + EPS) * kn_ref[...]
    k16_ref[...] = k.astype(MM)
    v = proj(2 * W_A, 3 * W_A)
    qi16_ref[...] = proj(3 * W_A, 4 * W_A).astype(MM)
    o = 4 * W_A
    sm = proj(o, o + LANES)
    sm32_ref[...] = sm
    sm16_ref[...] = sm.astype(MM)
    if feature_major:
        vt = v.T
        k32_ref[0] = k.T
        v32_ref[0] = vt
        v16_ref[0] = vt.astype(MM)
        smt_ref[0][...] = sm.T
    else:
        k32_ref[...] = k
        v32_ref[...] = v
        v16_ref[...] = v.astype(MM)
    o += LANES
    bqkv_ref[...] = proj(o, o + CB)
    bz_ref[...] = proj(o + CB, o + CB + W_B)
    o += CB + W_B
    cz_ref[...] = proj(o, o + W_C)
    cxbc_ref[...] = proj(o + W_C, o + W_C + CC)


def _arrange_w_in(w):
    w = w.T
    o_aik = 4 * W_A
    o_aiw = o_aik + D_I
    o_bqkv = o_aiw + H_I
    o_bz = o_bqkv + CB
    o_bbeta = o_bz + W_B
    o_ba = o_bbeta + H_B
    o_cz = o_ba + H_B
    o_cxbc = o_cz + W_C
    o_cdt = o_cxbc + CC
    small = jnp.concatenate(
        [w[o_aik:o_aiw], w[o_aiw:o_bqkv], w[o_bbeta:o_ba], w[o_ba:o_cz], w[o_cdt:o_cdt + H_C]], axis=0)
    small = jnp.pad(small, ((0, LANES - small.shape[0]), (0, 0)))
    return jnp.concatenate([w[:o_aik], small, w[o_bqkv:o_bbeta], w[o_cz:o_cdt]], axis=0).astype(MM)


def _inproj(h, g, w_arr, qn, kn, tm, feature_major_seq=None):
    n, d = h.shape
    wtot = w_arr.shape[0]
    bd = _block_diag(W_A, DH_A, 1.0 / DH_A)
    row = lambda w: pl.BlockSpec((tm, w), lambda i: (i, 0))
    rowout = lambda w, t: (row(w), jax.ShapeDtypeStruct((n, w), t))
    outs = [rowout(W_A, MM), rowout(W_A, F32), rowout(W_A, MM), rowout(W_A, F32), rowout(W_A, MM),
            rowout(W_A, MM), rowout(LANES, F32), rowout(LANES, MM), rowout(CB, F32), rowout(W_B, F32),
            rowout(W_C, F32), rowout(CC, F32)]
    if feature_major_seq is not None:
        tpb = feature_major_seq // tm
        per_seq = (pl.BlockSpec((1, W_A, tm), lambda i: (i // tpb, 0, i % tpb)),
                   jax.ShapeDtypeStruct((n // feature_major_seq, W_A, feature_major_seq), F32))
        outs[1] = per_seq
        outs[3] = per_seq
        outs[4] = (pl.BlockSpec((1, W_A, tm), lambda i: (i, 0, 0)),
                   jax.ShapeDtypeStruct((n // tm, W_A, tm), MM))
        outs.append((pl.BlockSpec((LANES, tm), lambda i: (0, i)), jax.ShapeDtypeStruct((LANES, n), F32)))
    return pl.pallas_call(
        functools.partial(_inproj_kernel, feature_major=feature_major_seq is not None),
        grid=(n // tm,),
        in_specs=[row(d), _full((1, d)), _full((wtot, d)), _full((1, W_A)), _full((1, W_A)),
                  _full((W_A, W_A))],
        out_specs=[s for s, _ in outs],
        out_shape=[t for _, t in outs],
        compiler_params=_params(("parallel",)),
        name="inproj",
    )(h, g.reshape(1, d), w_arr, jnp.tile(qn, H_A).reshape(1, W_A), jnp.tile(kn, H_A).reshape(1, W_A), bd)


def _mix_prologue(h_ref, ya_ref, yb_ref, yc_ref, wo_ref, gf_ref, h1_ref, u2_ref, acc_ref):
    h1 = h_ref[...] + (_dot(ya_ref[...], wo_ref[0:W_A, :])
                       + _dot(yb_ref[...], wo_ref[W_A:W_A + W_B, :])
                       + _dot(yc_ref[...], wo_ref[W_A + W_B:, :]))
    h1_ref[...] = h1
    u2 = _rms(h1, gf_ref[...])
    u2_ref[...] = u2.astype(MM)
    acc_ref[...] = jnp.zeros_like(acc_ref)
    return u2


def _mix_epilogue(p_ref, gp_ref, wg_ref, wp_ref, h1_ref, acc_ref, o_ref):
    h2 = h1_ref[...] + acc_ref[...]
    u3 = _rms(h2, gp_ref[...]).astype(MM)
    gate = _sigmoid(_dot(u3, wg_ref[...]))
    o_ref[...] = h2 + gate * _dot(p_ref[...].astype(MM), wp_ref[...])


def _ffn_kernel(h_ref, ya_ref, yb_ref, yc_ref, p_ref, wo_ref, gf_ref, w1_ref, w3_ref, w2_ref,
                gp_ref, wg_ref, wp_ref, o_ref, h1_ref, u2_ref, acc_ref):
    f = pl.program_id(1)

    @pl.when(f == 0)
    def _():
        _mix_prologue(h_ref, ya_ref, yb_ref, yc_ref, wo_ref, gf_ref, h1_ref, u2_ref, acc_ref)

    u2 = u2_ref[...]
    hid = _silu(_dot(u2, w1_ref[...])) * _dot(u2, w3_ref[...])
    acc_ref[...] += _dot(hid.astype(MM), w2_ref[...])

    @pl.when(f == pl.num_programs(1) - 1)
    def _():
        _mix_epilogue(p_ref, gp_ref, wg_ref, wp_ref, h1_ref, acc_ref, o_ref)


def _moe_kernel(h_ref, ya_ref, yb_ref, yc_ref, p_ref, wo_ref, gf_ref, wr_ref, w1_ref, w3_ref, w2_ref,
                gp_ref, wg_ref, wp_ref, o_ref, h1_ref, u2_ref, acc_ref, comb_ref):
    e = pl.program_id(1)
    f = pl.program_id(2)

    @pl.when((e == 0) & (f == 0))
    def _():
        u2 = _mix_prologue(h_ref, ya_ref, yb_ref, yc_ref, wo_ref, gf_ref, h1_ref, u2_ref, acc_ref)
        logits = _dot_hi(u2, wr_ref[...])
        lane = lax.broadcasted_iota(I32, logits.shape, 1)
        big = jnp.int32(LANES)
        lg = jnp.where(lane < N_EXPERTS, logits, -jnp.inf)
        m1 = jnp.max(lg, axis=1, keepdims=True)
        i1 = jnp.min(jnp.where(lg == m1, lane, big), axis=1, keepdims=True)
        lg2 = jnp.where(lane == i1, -jnp.inf, lg)
        m2 = jnp.max(lg2, axis=1, keepdims=True)
        i2 = jnp.min(jnp.where(lg2 == m2, lane, big), axis=1, keepdims=True)
        e2 = jnp.exp(m2 - m1)
        g1 = 1.0 / (1.0 + e2)
        g2 = e2 / (1.0 + e2)
        for ex in range(N_EXPERTS):
            ce = jnp.where(i1 == ex, g1, 0.0) + jnp.where(i2 == ex, g2, 0.0)
            comb_ref[ex] = jnp.broadcast_to(ce, comb_ref.shape[1:])

    u2 = u2_ref[...]
    hid = _silu(_dot(u2, w1_ref[0])) * _dot(u2, w3_ref[0])
    tf = hid.shape[1]
    ce = comb_ref[e]
    hid = hid * jnp.concatenate([ce] * (tf // LANES), axis=1)
    acc_ref[...] += _dot(hid.astype(MM), w2_ref[0])

    @pl.when((e == pl.num_programs(1) - 1) & (f == pl.num_programs(2) - 1))
    def _():
        _mix_epilogue(p_ref, gp_ref, wg_ref, wp_ref, h1_ref, acc_ref, o_ref)


def _channel_mix(h, ya, yb, yc, p, wo, gf, gp, wg, wp, tm, tf, *, ffn=None, moe=None):
    n, d = h.shape
    dple = p.shape[1]
    if ffn is not None:
        w1, w3, w2 = ffn
        dff = w1.shape[1]
        grid = (n // tm, dff // tf)
        row = lambda w: pl.BlockSpec((tm, w), lambda i, f: (i, 0))
        wspecs = [pl.BlockSpec((d, tf), lambda i, f: (0, f)), pl.BlockSpec((d, tf), lambda i, f: (0, f)),
                  pl.BlockSpec((tf, d), lambda i, f: (f, 0))]
        wargs = [w1, w3, w2]
        body = _ffn_kernel
        extra_scratch = []
        sem = ("parallel", "arbitrary")
    else:
        wr, w1, w3, w2 = moe
        ne, _, dff = w1.shape
        grid = (n // tm, ne, dff // tf)
        row = lambda w: pl.BlockSpec((tm, w), lambda i, e, f: (i, 0))
        wspecs = [_full((d, LANES)),
                  pl.BlockSpec((1, d, tf), lambda i, e, f: (e, 0, f)),
                  pl.BlockSpec((1, d, tf), lambda i, e, f: (e, 0, f)),
                  pl.BlockSpec((1, tf, d), lambda i, e, f: (e, f, 0))]
        wargs = [jnp.pad(wr, ((0, 0), (0, LANES - ne))), w1, w3, w2]
        body = _moe_kernel
        extra_scratch = [pltpu.VMEM((ne, tm, LANES), F32)]
        sem = ("parallel", "arbitrary", "arbitrary")
    return pl.pallas_call(
        body,
        grid=grid,
        in_specs=[row(d), row(W_A), row(W_B), row(W_C), row(dple), _full((d, d)), _full((1, d))]
                 + wspecs + [_full((1, d)), _full((d, d)), _full((dple, d))],
        out_specs=row(d),
        out_shape=jax.ShapeDtypeStruct((n, d), F32),
        scratch_shapes=[pltpu.VMEM((tm, d), F32), pltpu.VMEM((tm, d), MM), pltpu.VMEM((tm, d), F32)]
                       + extra_scratch,
        compiler_params=_params(sem),
        name="ffn" if ffn is not None else "moe",
    )(h, ya, yb, yc, p, wo, gf.reshape(1, d), *wargs, gp.reshape(1, d), wg, wp)


def _sortable(x):
    b = lax.bitcast_convert_type(x, I32)
    return b ^ ((b >> 31) & jnp.int32(0x7FFFFFFF))


def _select_topk(keys_ref, bias_ref, n_groups, group, topk, idx_bits, tile_shape, key_axis):
    imin = jnp.int32(INT_MIN)
    pos = lax.broadcasted_iota(I32, tile_shape, key_axis)
    if key_axis == 1:
        qshape = tile_shape
        full = lambda x: x
        acc0 = jnp.zeros(tile_shape, F32)
        add_tile = lambda acc, m: acc + m
        finish = lambda acc: jnp.broadcast_to(jnp.sum(acc, axis=1, keepdims=True), tile_shape)
    else:
        qshape = (1, tile_shape[1])
        full = lambda x: jnp.broadcast_to(x, tile_shape)
        acc0 = jnp.zeros((SUBLANES, tile_shape[1]), F32)

        def add_tile(acc, m):
            parts = [m[s * SUBLANES:(s + 1) * SUBLANES] for s in range(tile_shape[0] // SUBLANES)]
            while len(parts) > 1:
                parts = [a + b for a, b in zip(parts[::2], parts[1::2])]
            return acc + parts[0]

        finish = lambda acc: jnp.sum(acc, axis=0, keepdims=True)

    def count(pred):
        def body(i, acc):
            for t in range(group):
                j = i * group + t
                acc = add_tile(acc, jnp.where(pred(keys_ref[j], j), 1.0, 0.0))
            return acc
        return finish(lax.fori_loop(0, n_groups, body, acc0))

    def value_bit(b, carry):
        thr, cnt_thr = carry
        cand = thr + jnp.left_shift(jnp.int32(1), 31 - b)
        cand_f = full(cand)
        cnt = count(lambda k, j: k >= cand_f)
        keep = cnt >= topk
        return jnp.where(keep, cand, thr), jnp.where(keep, cnt, cnt_thr)

    thr, cnt_thr = lax.fori_loop(0, 32, value_bit, (jnp.full(qshape, imin), jnp.full(qshape, 2.0 ** 30, F32)))
    no_cut = jnp.full(qshape, 2 ** idx_bits - 1, I32)
    tied = (cnt_thr > topk) & (thr > imin)
    floor = jnp.maximum(thr, imin + 1)
    floor_f = full(floor)

    def index_cut():
        need = topk - count(lambda k, j: k > floor_f)

        def index_bit(b, c):
            cc = c | jnp.left_shift(jnp.int32(1), idx_bits - 1 - b)
            cc_f = full(cc)
            cnt = count(lambda k, j: (k == floor_f) & (j * LANES + pos < cc_f))
            return jnp.where(cnt <= need, cc, c)

        return lax.fori_loop(0, idx_bits, index_bit, jnp.zeros(qshape, I32))

    cut = lax.cond(jnp.max(jnp.where(tied, 1.0, 0.0)) > 0.0, index_cut, lambda: no_cut)
    cut_f = full(jnp.where(tied, cut, no_cut))

    def write(i, carry):
        for t in range(group):
            j = i * group + t
            k = keys_ref[j]
            sel = (k > floor_f) | ((k == floor_f) & (j * LANES + pos < cut_f))
            bias_ref[j] = jnp.where(sel, 0.0, NEG)
        return carry

    lax.fori_loop(0, n_groups, write, 0)


def _two_buffer_pipeline(n, front, back, bufs):
    a, b = bufs
    front(0, a)

    def step(c, cur, nxt):
        @pl.when(c + 1 < n)
        def _():
            back(c, cur, lambda: front(c + 1, nxt))

        @pl.when(c + 1 == n)
        def _():
            back(c, cur, lambda: None)

    def body(k, carry):
        step(2 * k, a, b)
        step(2 * k + 1, b, a)
        return carry

    lax.fori_loop(0, (n + 1) // 2, body, 0)


def _dsa_prompt_kernel(q_ref, qi_ref, smt_ref, k_ref, vt_ref, ki_ref, o_ref,
                       keys_ref, bias_ref, qp_ref, qip_ref, m_ref, mb_ref, l_ref, acc_ref, sta_ref, stb_ref,
                       *, topk, kc, idx_bits):
    i = pl.program_id(1)
    qb = q_ref.shape[0]
    tpc = kc // LANES
    n_chunks = ((i + 1) * qb + kc - 1) // kc
    n_pairs = H_A // 2
    tile = (LANES, qb)
    key_pos = lax.broadcasted_iota(I32, tile, 0)
    q_pos = i * qb + lax.broadcasted_iota(I32, tile, 1)
    low = lax.broadcasted_iota(I32, (qb, LANES), 1) < DH_A

    for h in range(H_A):
        pr, half = h // 2, h % 2
        rows = slice(half * qb, (half + 1) * qb)
        blk = q_ref[:, pr * LANES:(pr + 1) * LANES]
        qp_ref[pr, rows, :] = jnp.where(low if half == 0 else ~low, blk, jnp.zeros_like(blk))
        blk = qi_ref[:, pr * LANES:(pr + 1) * LANES]
        if half == 1:
            blk = pltpu.roll(blk, DH_A, 1)
        qip_ref[pr, rows, :] = jnp.where(low, blk, jnp.zeros_like(blk))
    wt = smt_ref[SM_WI:SM_WI + H_I, :] * ((D_I ** -0.5) * (H_I ** -0.5))

    def score_chunk(c, carry):
        base = pl.multiple_of(c * kc, kc)
        kt = ki_ref[pl.ds(base, kc), :]
        ss = [_dot_nt(kt, qip_ref[pr]) for pr in range(n_pairs)]
        for t in range(tpc):
            sc = jnp.zeros(tile, F32)
            for h in range(H_I):
                blk = ss[h // 2][t * LANES:(t + 1) * LANES, (h % 2) * qb:(h % 2 + 1) * qb]
                sc = sc + wt[h:h + 1, :] * jnp.maximum(blk, 0.0)
            valid = base + t * LANES + key_pos <= q_pos
            keys_ref[c * tpc + t] = jnp.where(valid, _sortable(sc), jnp.int32(INT_MIN))
        return carry

    lax.fori_loop(0, n_chunks, score_chunk, 0)
    _select_topk(keys_ref, bias_ref, n_chunks, tpc, topk, idx_bits, tile, 0)

    m_ref[...] = jnp.full(m_ref.shape, NEG, F32)
    mb_ref[...] = jnp.full(mb_ref.shape, NEG, F32)
    l_ref[...] = jnp.zeros(l_ref.shape, F32)
    acc_ref[...] = jnp.zeros(acc_ref.shape, F32)

    def front(c, buf):
        base = pl.multiple_of(c * kc, kc)
        prods = [_dot_nt(k_ref[pl.ds(base, kc), pr * LANES:(pr + 1) * LANES], qp_ref[pr])
                 for pr in range(n_pairs)]
        for pr in range(n_pairs):
            st = []
            for t in range(tpc):
                b = bias_ref[c * tpc + t]
                x = prods[pr][t * LANES:(t + 1) * LANES, :] + jnp.concatenate([b, b], axis=1)
                buf[pr, t * LANES:(t + 1) * LANES, :] = x
                st.append(x)
            cmax = jnp.max(functools.reduce(jnp.maximum, st), axis=0, keepdims=True)
            m_ref[pr] = jnp.maximum(m_ref[pr], cmax)

    def back(c, buf, issue_next):
        m_now = [m_ref[pr] for pr in range(n_pairs)]
        issue_next()
        for pr in range(n_pairs):
            alpha = jnp.exp2(mb_ref[pr] - m_now[pr])
            ps = [jnp.exp2(buf[pr, t * LANES:(t + 1) * LANES, :] - m_now[pr]) for t in range(tpc)]
            l_ref[pr] = l_ref[pr] * alpha + jnp.sum(functools.reduce(jnp.add, ps), axis=0, keepdims=True)
            p = jnp.concatenate(ps, axis=0).astype(MM)
            acc_ref[pr] = acc_ref[pr] * alpha + _dot(vt_ref[c, pr * LANES:(pr + 1) * LANES, :], p)
            mb_ref[pr] = m_now[pr]

    _two_buffer_pipeline(n_chunks, front, back, (sta_ref, stb_ref))
    for pr in range(n_pairs):
        o = acc_ref[pr] / l_ref[pr]
        ot = jnp.concatenate([o[:DH_A, :qb], o[DH_A:, qb:]], axis=0)
        o_ref[:, pr * LANES:(pr + 1) * LANES] = ot.T.astype(o_ref.dtype)


def _dsa_prompt(q16, qi16, smt, k16, vt, sm16, batch, seq, kc):
    qb = Q_BLOCK
    topk = min(TOPK, seq // 4)
    nq = seq // qb
    idx_bits = int(seq).bit_length() + 1
    assert H_A == H_I and DH_A == D_I and qb == LANES
    n_kc = seq // kc
    n_pairs = H_A // 2
    blk = lambda w: pl.BlockSpec((qb, w), lambda b, i: (b * nq + i, 0))
    seqblk = lambda w: pl.BlockSpec((seq, w), lambda b, i: (b, 0), pipeline_mode=pl.Buffered(1))
    tiles = (seq // LANES, LANES, qb)
    return pl.pallas_call(
        functools.partial(_dsa_prompt_kernel, topk=topk, kc=kc, idx_bits=idx_bits),
        grid=(batch, nq),
        in_specs=[blk(W_A), blk(W_A), pl.BlockSpec((LANES, qb), lambda b, i: (0, b * nq + i)), seqblk(W_A),
                  pl.BlockSpec((n_kc, W_A, kc), lambda b, i: (b, 0, 0), pipeline_mode=pl.Buffered(1)),
                  seqblk(LANES)],
        out_specs=blk(W_A),
        out_shape=jax.ShapeDtypeStruct((batch * seq, W_A), MM),
        scratch_shapes=[pltpu.VMEM(tiles, I32), pltpu.VMEM(tiles, F32),
                        pltpu.VMEM((n_pairs, 2 * qb, LANES), MM), pltpu.VMEM((n_pairs, 2 * qb, LANES), MM),
                        pltpu.VMEM((n_pairs, 1, 2 * qb), F32), pltpu.VMEM((n_pairs, 1, 2 * qb), F32),
                        pltpu.VMEM((n_pairs, 1, 2 * qb), F32), pltpu.VMEM((n_pairs, LANES, 2 * qb), F32),
                        pltpu.VMEM((n_pairs, kc, 2 * qb), F32), pltpu.VMEM((n_pairs, kc, 2 * qb), F32)],
        compiler_params=_params(("parallel", "arbitrary")),
        name="dsa_prompt",
    )(q16, qi16, smt, k16, vt, sm16)


def _expand_matrix(first_lane, heads, width):
    m = np.zeros((LANES, heads * width), np.float32)
    for h in range(heads):
        m[first_lane + h, h * width:(h + 1) * width] = 1.0
    return jnp.asarray(m, MM)


def _chunk_masks(tt, chunk):
    i = np.arange(tt)
    same = (i[:, None] // chunk) == (i[None, :] // chunk)
    ltri = jnp.asarray(np.where(same & (i[:, None] >= i[None, :]), 1.0, 0.0), MM)
    utri = jnp.asarray(np.where(same & (i[:, None] <= i[None, :]), 1.0, 0.0), F32)
    return ltri, utri


def _lane_vec(values, first_lane):
    return jnp.zeros((1, LANES), F32).at[0, first_lane:first_lane + values.shape[0]].set(values)


def _conv_tile(x_ref, buf_ref, cw_ref, xe_ref, nb_ref, t, n_t, tv_last):
    tt = x_ref.shape[0]
    hist = SUBLANES

    @pl.when(t == 0)
    def _():
        xe_ref[0:hist, :] = jnp.zeros((hist, xe_ref.shape[1]), F32)
        xe_ref[hist - (CONV_W - 1):hist, :] = buf_ref[0]

    xe_ref[hist:hist + tt, :] = x_ref[...]
    y = jnp.zeros(x_ref.shape, F32)
    for i in range(CONV_W):
        o = hist - (CONV_W - 1) + i
        y = y + xe_ref[o:o + tt, :] * cw_ref[i:i + 1, :]

    @pl.when(t == n_t - 1)
    def _():
        nb_ref[0] = xe_ref[hist + tv_last - (CONV_W - 1):hist + tv_last, :]

    xe_ref[0:hist, :] = xe_ref[tt:tt + hist, :]
    return y


def _live_rows(shape, t, n_t, tv_last):
    row = lax.broadcasted_iota(I32, shape, 0)
    return (t < n_t - 1) | (row < tv_last)


def _row_cumsum(vals, utri, ones_c, heads, width, chunk):
    tt = vals.shape[0]
    out = []
    for h in range(heads):
        col = vals[:, h * width:h * width + chunk]
        out.append(_x3_dot(ones_c, jnp.concatenate([col] * (tt // chunk), axis=1) * utri))
    return out


def _gdn_kernel(x_ref, z_ref, sm_ref, buf_ref, s0_ref, cw_ref, alog_ref, dtb_ref, ng_ref,
                ebeta_ref, eg_ref, ltri_ref, utri_ref, bd1_ref, bdn_ref,
                y_ref, nb_ref, st_ref, xe_ref, s_ref, o_ref, *, tv_last, chunk):
    t = pl.program_id(1)
    n_t = pl.num_programs(1)
    tt = x_ref.shape[0]
    wq = H_B * DK_B

    @pl.when(t == 0)
    def _():
        s_ref[...] = s0_ref[0]

    qkv = _silu(_conv_tile(x_ref, buf_ref, cw_ref, xe_ref, nb_ref, t, n_t, tv_last))
    qk = qkv[:, :2 * wq]
    qk = qk * lax.rsqrt(_group_mean(qk * qk, bd1_ref[...]) + EPS)
    q = qk[:, :wq] * (DK_B ** -0.5)
    k = qk[:, wq:]
    v = qkv[:, 2 * wq:]

    sm = sm_ref[...]
    beta = _sigmoid(sm)
    g = -jnp.exp(alog_ref[...]) * _softplus(sm + dtb_ref[...])
    if tv_last < tt:
        live = _live_rows(sm.shape, t, n_t, tv_last)
        beta = jnp.where(live, beta, 0.0)
        g = jnp.where(live, g, 0.0)
    betab = _dot_x3(beta, ebeta_ref[...])
    gb = _dot_x3(g, eg_ref[...])
    gc = _x3_dot(ltri_ref[...], gb)
    ones_c = jnp.ones((chunk, tt), MM)
    gr = _row_cumsum(gb, utri_ref[...], ones_c, H_B, DK_B, chunk)
    eg = jnp.exp(gc)
    kb = k * betab
    vb = v * betab
    kbeg = kb * eg
    qg = q * eg
    ii = lax.broadcasted_iota(I32, (chunk, chunk), 0)
    jj = lax.broadcasted_iota(I32, (chunk, chunk), 1)
    levels = max(1, (chunk - 1).bit_length())

    n_c = tt // chunk
    units = [(c, h) for c in range(n_c) for h in range(H_B)]
    rows = lambda c: slice(c * chunk, (c + 1) * chunk)
    cols = lambda h: slice(h * DK_B, (h + 1) * DK_B)
    k16 = k.astype(MM)
    kk = {u: _dot_nt(kb[rows(u[0]), cols(u[1])].astype(MM), k16[rows(u[0]), cols(u[1])]) for u in units}
    qk = {u: _dot_nt(q[rows(u[0]), cols(u[1])].astype(MM), k16[rows(u[0]), cols(u[1])]) for u in units}
    xs, ps, attn = {}, {}, {}
    for u in units:
        c, h = u
        decay = jnp.exp(jnp.where(ii >= jj, gc[rows(c), cols(h)] - gr[h][:, rows(c)], NEG))
        ps[u] = -jnp.where(ii > jj, kk[u] * decay, 0.0)
        attn[u] = (qk[u] * decay).astype(MM)
        xs[u] = jnp.concatenate([vb[rows(c), cols(h)], kbeg[rows(c), cols(h)]], axis=1)
    wx = 2 * DV_B
    for lv in range(levels):
        last_lv = lv == levels - 1
        prod = {u: _dot_hi(ps[u], xs[u] if last_lv else jnp.concatenate([xs[u], ps[u]], axis=1)) for u in units}
        for u in units:
            xs[u] = xs[u] + prod[u][:, :wx]
            if not last_lv:
                ps[u] = prod[u][:, wx:]

    for c in range(n_c):
        last = gc[(c + 1) * chunk - 1:(c + 1) * chunk, :]
        kg = (k[rows(c), :] * jnp.exp(last - gc[rows(c), :])).astype(MM)
        gl = jnp.exp(last)
        qg16 = qg[rows(c), :].astype(MM)
        s_old = [s_ref[h] for h in range(H_B)]
        s16 = [s.astype(MM) for s in s_old]
        ws = [_dot(xs[c, h][:, DV_B:].astype(MM), s16[h]) for h in range(H_B)]
        qs = [_dot(qg16[:, cols(h)], s16[h]) for h in range(H_B)]
        u16 = [(xs[c, h][:, :DV_B] - ws[h]).astype(MM) for h in range(H_B)]
        au = [_dot(attn[c, h], u16[h]) for h in range(H_B)]
        ku = [_dot_tn(kg[:, cols(h)], u16[h]) for h in range(H_B)]
        for h in range(H_B):
            o_ref[rows(c), cols(h)] = qs[h] + au[h]
            s_ref[h] = s_old[h] * gl[:, h * DK_B:h * DK_B + 1] + ku[h]

    o = o_ref[...]
    o = o * lax.rsqrt(_group_mean(o * o, bdn_ref[...]) + EPS) * ng_ref[...]
    y_ref[...] = (o * _silu(z_ref[...])).astype(y_ref.dtype)

    @pl.when(t == n_t - 1)
    def _():
        st_ref[0] = s_ref[...]


def _ssd_kernel(x_ref, z_ref, sm_ref, buf_ref, h0_ref, cw_ref, cb_ref, alog_ref, dtb_ref, d_ref, ng_ref,
                edt_ref, ltri_ref, utri_ref, bdn_ref,
                y_ref, nb_ref, ht_ref, xe_ref, h_ref, o_ref, *, tv_last, chunk):
    t = pl.program_id(1)
    n_t = pl.num_programs(1)
    tt = x_ref.shape[0]
    gw = G_C * N_C
    hpg = H_C // G_C

    @pl.when(t == 0)
    def _():
        h_ref[...] = h0_ref[0]

    xbc = _silu(_conv_tile(x_ref, buf_ref, cw_ref, xe_ref, nb_ref, t, n_t, tv_last) + cb_ref[...])
    xs = xbc[:, :W_C]
    bm = xbc[:, W_C:W_C + gw].astype(MM)
    cm = xbc[:, W_C + gw:].astype(MM)

    dt = _softplus(sm_ref[...] + dtb_ref[...])
    if tv_last < tt:
        dt = jnp.where(_live_rows(dt.shape, t, n_t, tv_last), dt, 0.0)
    dtb = _dot_x3(dt, edt_ref[...])
    a = dtb * (-jnp.exp(alog_ref[...]))
    xdt = xs * dtb
    ac = _x3_dot(ltri_ref[...], a)
    ones_c = jnp.ones((chunk, tt), MM)
    ar = _row_cumsum(a, utri_ref[...], ones_c, H_C, P_C, chunk)
    ea = jnp.exp(ac)
    ii = lax.broadcasted_iota(I32, (chunk, chunk), 0)
    jj = lax.broadcasted_iota(I32, (chunk, chunk), 1)

    for c in range(tt // chunk):
        rs = slice(c * chunk, (c + 1) * chunk)
        last = ac[(c + 1) * chunk - 1:(c + 1) * chunk, :]
        xdec = (xdt[rs, :] * jnp.exp(last - ac[rs, :])).astype(MM)
        hdec = jnp.exp(last)
        for g in range(G_C):
            gs = slice(g * N_C, (g + 1) * N_C)
            cbm = _dot_nt(cm[rs, gs], bm[rs, gs])
            for h in range(g * hpg, (g + 1) * hpg):
                hs = slice(h * P_C, (h + 1) * P_C)
                lmat = jnp.exp(jnp.where(ii >= jj, ac[rs, hs] - ar[h][:, rs], NEG))
                hprev = h_ref[h]
                y = _dot((cbm * lmat).astype(MM), xdt[rs, hs].astype(MM))
                y = y + ea[rs, hs] * _dot_nt(cm[rs, gs], hprev.astype(MM))
                o_ref[rs, hs] = y + d_ref[:, hs] * xs[rs, hs]
                h_ref[h] = hprev * hdec[:, h * P_C:h * P_C + 1] + _dot_tn(xdec[:, hs], bm[rs, gs])

    yz = o_ref[...] * _silu(z_ref[...])
    y_ref[...] = (yz * lax.rsqrt(_group_mean(yz * yz, bdn_ref[...]) + EPS) * ng_ref[...]).astype(y_ref.dtype)

    @pl.when(t == n_t - 1)
    def _():
        ht_ref[0] = h_ref[...]


def _recurrent_call(body, name, x, z, sm, buf, state0, vecs, consts, batch, t_pad, tt, tv_last, width_out):
    n_t = t_pad // tt
    cw = x.shape[1]
    blk = lambda w: pl.BlockSpec((tt, w), lambda b, t: (b * n_t + t, 0))
    per_b = lambda shp: pl.BlockSpec((1,) + shp, lambda b, t: (b,) + (0,) * len(shp))
    sshape = state0.shape[1:]
    return pl.pallas_call(
        functools.partial(body, tv_last=tv_last, chunk=min(CHUNK, tt)),
        grid=(batch, n_t),
        in_specs=[blk(cw), blk(width_out), blk(LANES), per_b((CONV_W - 1, cw)), per_b(sshape)]
                 + [_full(v.shape) for v in vecs] + [_full(c.shape) for c in consts],
        out_specs=[blk(width_out), per_b((CONV_W - 1, cw)), per_b(sshape)],
        out_shape=[jax.ShapeDtypeStruct((batch * t_pad, width_out), MM),
                   jax.ShapeDtypeStruct((batch, CONV_W - 1, cw), F32),
                   jax.ShapeDtypeStruct((batch,) + sshape, F32)],
        scratch_shapes=[pltpu.VMEM((tt + SUBLANES, cw), F32), pltpu.VMEM(sshape, F32),
                        pltpu.VMEM((tt, width_out), F32)],
        compiler_params=_params(("parallel", "arbitrary")),
        name=name,
    )(x, z, sm, buf, state0, *vecs, *consts)


def _gdn(x, z, sm, buf, s0, conv_w, a_log, dt_bias, norm_g, batch, t_pad, tt, tv_last):
    ltri, utri = _chunk_masks(tt, min(CHUNK, tt))
    vecs = [conv_w, _lane_vec(a_log, SM_A), _lane_vec(dt_bias, SM_A), jnp.tile(norm_g, H_B).reshape(1, W_B)]
    consts = [_expand_matrix(SM_BETA, H_B, DK_B), _expand_matrix(SM_A, H_B, DK_B), ltri, utri,
              _block_diag(2 * H_B * DK_B, DK_B, 1.0), _block_diag(W_B, DV_B, 1.0 / DV_B)]
    return _recurrent_call(_gdn_kernel, "gdn", x, z, sm, buf, s0, vecs, consts, batch, t_pad, tt, tv_last, W_B)


def _ssd(x, z, sm, buf, h0, conv_w, conv_b, a_log, dt_bias, dskip, norm_g, batch, t_pad, tt, tv_last):
    ltri, utri = _chunk_masks(tt, min(CHUNK, tt))
    vecs = [conv_w, conv_b.reshape(1, CC), jnp.repeat(a_log, P_C).reshape(1, W_C), _lane_vec(dt_bias, SM_DT),
            jnp.repeat(dskip, P_C).reshape(1, W_C), norm_g.reshape(1, W_C)]
    consts = [_expand_matrix(SM_DT, H_C, P_C), ltri, utri, _block_diag(W_C, W_C // G_C, float(G_C) / W_C)]
    return _recurrent_call(_ssd_kernel, "ssd", x, z, sm, buf, h0, vecs, consts, batch, t_pad, tt, tv_last, W_C)


def _dsa_sample_select_kernel(pt_ref, qi_ref, w_ref, kin_ref, *rest, topk, n_q, idx_bits, pg, group):
    pages = rest[:pg]
    bias_ref, keys_ref = rest[pg:]
    j = pl.program_id(1)
    n_s = pl.num_programs(1)
    n_tiles = bias_ref.shape[1]
    shape = (SUBLANES, LANES)
    row = lax.broadcasted_iota(I32, shape, 0)
    lane = lax.broadcasted_iota(I32, shape, 1)
    imin = jnp.int32(INT_MIN)

    def keys_of(s, valid):
        r = jnp.maximum(s, 0.0) * w_ref[0]
        per_q = [jnp.sum(r[q * H_I:(q + 1) * H_I], axis=0, keepdims=True) for q in range(n_q)]
        sc = jnp.concatenate(per_q + [jnp.zeros((SUBLANES - n_q, LANES), F32)], axis=0)
        return jnp.where(valid, _sortable(sc), imin)

    qi = qi_ref[0]
    ss = [_dot(qi, pages[t][0, 0].astype(MM)) for t in range(pg)]
    for t in range(pg):
        keys_ref[j * pg + t] = keys_of(ss[t], row < n_q)

    @pl.when(j == n_s - 1)
    def _():
        keys_ref[n_tiles - 1] = keys_of(_dot(qi, kin_ref[0]), (row < n_q) & (lane <= row))
        _select_topk(keys_ref, bias_ref.at[0], n_tiles // group, group, topk, idx_bits, shape, 1)


def _dsa_sample_attend_kernel(pt_ref, q_ref, bias_ref, bnew_ref, kn_ref, vn_ref, *rest, pg):
    kps, vps = rest[:pg], rest[pg:2 * pg]
    o_ref, qbd_ref, m_ref, l_ref, acc_ref = rest[2 * pg:]
    j = pl.program_id(1)
    n_s = pl.num_programs(1)
    rows = H_A * SUBLANES
    shape = (rows, LANES)
    head_of_row = lax.broadcasted_iota(I32, (rows, W_A), 0) // SUBLANES
    head_of_lane = lax.broadcasted_iota(I32, (rows, W_A), 1) // DH_A
    own = head_of_row == head_of_lane

    @pl.when(j == 0)
    def _():
        qrep = jnp.concatenate([q_ref[0]] * H_A, axis=0)
        qbd_ref[...] = jnp.where(own, qrep, jnp.zeros_like(qrep))
        m_ref[...] = jnp.full(shape, NEG, F32)
        l_ref[...] = jnp.zeros(shape, F32)
        acc_ref[...] = jnp.zeros((rows, W_A), F32)

    def absorb(kks, vvs, biases):
        qbd = qbd_ref[...]
        ss = [_dot(qbd, kk) for kk in kks]
        bs = [jnp.concatenate([b] * H_A, axis=0) for b in biases]
        ss = [s + b for s, b in zip(ss, bs)]
        m_old = m_ref[...]
        mx = jnp.max(functools.reduce(jnp.maximum, ss), axis=1, keepdims=True)
        m_new = jnp.maximum(m_old, jnp.broadcast_to(mx, shape))
        alpha = jnp.exp2(m_old - m_new)
        ps = [jnp.where(b == 0.0, jnp.exp2(s - m_new), 0.0) for s, b in zip(ss, bs)]
        psum = jnp.sum(functools.reduce(jnp.add, ps), axis=1, keepdims=True)
        l_ref[...] = l_ref[...] * alpha + jnp.broadcast_to(psum, shape)
        pv = functools.reduce(jnp.add, [_dot_nt(p.astype(MM), vv) for p, vv in zip(ps, vvs)])
        acc_ref[...] = acc_ref[...] * jnp.concatenate([alpha] * (W_A // LANES), axis=1) + pv
        m_ref[...] = m_new

    absorb([kp[0, 0].astype(MM) for kp in kps], [vp[0, 0].astype(MM) for vp in vps],
           [bias_ref[0, t] for t in range(pg)])

    @pl.when(j == n_s - 1)
    def _():
        absorb([kn_ref[0]], [vn_ref[0]], [bnew_ref[0, 0]])
        o = acc_ref[...] / jnp.concatenate([l_ref[...]] * (W_A // LANES), axis=1)
        o = jnp.where(own, o, 0.0)
        out = functools.reduce(jnp.add, [o[h * SUBLANES:(h + 1) * SUBLANES] for h in range(H_A)])
        o_ref[0] = out.astype(o_ref.dtype)


def _dsa_sample(layer, q16, qi16, sm32, sm16, k16, v16, cache_k, cache_v, cache_ki, page_table, bs, ts):
    n_pages = page_table.shape[1]
    page = cache_ki.shape[2]
    assert page == LANES and ts <= SUBLANES
    past = n_pages * page
    topk = min(TOPK, (past + ts) // 4)
    idx_bits = int(past + ts).bit_length() + 1
    pt = page_table.reshape(-1)
    rq = ts * H_I
    qi_r = qi16.reshape(bs, rq, D_I)
    wi = sm32[:, SM_WI:SM_WI + H_I] * ((D_I ** -0.5) * (H_I ** -0.5))
    w_r = jnp.broadcast_to(wi.reshape(bs, rq, 1), (bs, rq, LANES))
    pad_rows = lambda a, rows: jnp.pad(a.reshape(bs, ts, -1), ((0, 0), (0, rows - ts), (0, 0)))
    new_t = lambda a: jnp.pad(jnp.swapaxes(a.reshape(bs, ts, -1), 1, 2), ((0, 0), (0, 0), (0, page - ts)))
    kin = new_t(sm16[:, SM_KI:SM_KI + D_I])
    per_b = lambda shp: pl.BlockSpec((1,) + shp, lambda b, j, p: (b,) + (0,) * len(shp))
    pg = math.gcd(n_pages, 8)
    n_steps = n_pages // pg
    paged = lambda w, t: pl.BlockSpec(
        (1, 1, w, page), lambda b, j, p: (layer, p[b * n_pages + j * pg + t], 0, 0))
    tiles = n_pages + 1
    group = max(g for g in range(1, 9) if tiles % g == 0)

    bias = pl.pallas_call(
        functools.partial(_dsa_sample_select_kernel, topk=topk, n_q=ts, idx_bits=idx_bits, pg=pg, group=group),
        grid_spec=pltpu.PrefetchScalarGridSpec(
            num_scalar_prefetch=1, grid=(bs, n_steps),
            in_specs=[per_b((rq, D_I)), per_b((rq, LANES)), per_b((D_I, page))]
                     + [paged(D_I, t) for t in range(pg)],
            out_specs=per_b((tiles, SUBLANES, LANES)),
            scratch_shapes=[pltpu.VMEM((tiles, SUBLANES, LANES), I32)]),
        out_shape=jax.ShapeDtypeStruct((bs, tiles, SUBLANES, LANES), F32),
        compiler_params=_params(("parallel", "arbitrary")),
        name="dsa_sample_select",
    )(pt, qi_r, w_r, kin, *([jnp.swapaxes(cache_ki, 2, 3)] * pg))

    page_t = lambda c: jnp.transpose(c, (0, 1, 3, 4, 2)).reshape(c.shape[:2] + (W_A, page))
    ck = page_t(cache_k)
    cv = page_t(cache_v)
    rows = H_A * SUBLANES
    out = pl.pallas_call(
        functools.partial(_dsa_sample_attend_kernel, pg=pg),
        grid_spec=pltpu.PrefetchScalarGridSpec(
            num_scalar_prefetch=1, grid=(bs, n_steps),
            in_specs=[per_b((SUBLANES, W_A)),
                      pl.BlockSpec((1, pg, SUBLANES, LANES), lambda b, j, p: (b, j, 0, 0)),
                      pl.BlockSpec((1, 1, SUBLANES, LANES), lambda b, j, p: (b, n_pages, 0, 0)),
                      per_b((W_A, page)), per_b((W_A, page))]
                     + [paged(W_A, t) for t in range(pg)] * 2,
            out_specs=per_b((SUBLANES, W_A)),
            scratch_shapes=[pltpu.VMEM((rows, W_A), MM), pltpu.VMEM((rows, LANES), F32),
                            pltpu.VMEM((rows, LANES), F32), pltpu.VMEM((rows, W_A), F32)]),
        out_shape=jax.ShapeDtypeStruct((bs, SUBLANES, W_A), MM),
        compiler_params=_params(("parallel", "arbitrary")),
        name="dsa_sample_attend",
    )(pt, pad_rows(q16, SUBLANES), bias, bias, new_t(k16), new_t(v16), *([ck] * pg), *([cv] * pg))
    return out[:, :ts].reshape(bs * ts, W_A)


def _pad_seq(a, batch, t, t_pad):
    if t == t_pad:
        return a
    return jnp.pad(a.reshape(batch, t, -1), ((0, 0), (0, t_pad - t), (0, 0))).reshape(batch * t_pad, -1)


def _trunk(x, p, lw, conv_b0, ssm_b0, conv_c0, ssm_c0, kv, tm, tt):
    batch, t, d = x.shape
    n = batch * t
    depth = len(lw)
    t_pad = -(-t // tt) * tt
    tv_last = t - (t_pad - tt)
    assert min(CHUNK, tt) == DK_B == P_C and tv_last >= CONV_W - 1
    h = x.reshape(n, d)
    rec = [[] for _ in range(7)]
    for l, w in enumerate(lw):
        if kv is None:
            assert t % tm == 0
            (q16, k32, k16, v32, v16t, qi16, sm32, sm16, bqkv, bz, cz, cxbc, smt) = _inproj(
                h, w["g_mix"], w["w_in"], w["q_norm"], w["k_norm"], tm, feature_major_seq=t)
            ya = _dsa_prompt(q16, qi16, smt, k16, v16t, sm16, batch, t, tm)
            heads_last = lambda a: jnp.transpose(a.reshape(batch, H_A, DH_A, t), (0, 3, 1, 2))
            k_rows, v_rows = heads_last(k32), heads_last(v32)
        else:
            (q16, k32, k16, v32, v16, qi16, sm32, sm16, bqkv, bz, cz, cxbc) = _inproj(
                h, w["g_mix"], w["w_in"], w["q_norm"], w["k_norm"], tm)
            ya = _dsa_sample(l, q16, qi16, sm32, sm16, k16, v16, kv[0], kv[1], kv[2], kv[3], batch, t)
            k_rows, v_rows = k32.reshape(batch, t, H_A, DH_A), v32.reshape(batch, t, H_A, DH_A)
        pad = lambda a: _pad_seq(a, batch, t, t_pad)
        unpad = lambda a: a if t == t_pad else a.reshape(batch, t_pad, -1)[:, :t].reshape(n, -1)
        smp = pad(sm32)
        yb, bconv, bs_ = _gdn(pad(bqkv), pad(bz), smp, conv_b0[l], ssm_b0[l], w["b_conv_w"], w["b_A_log"],
                              w["b_dt_bias"], w["b_norm"], batch, t_pad, tt, tv_last)
        yc, cconv, cs_ = _ssd(pad(cxbc), pad(cz), smp, conv_c0[l], ssm_c0[l], w["c_conv_w"], w["c_conv_b"],
                              w["c_A_log"], w["c_dt_bias"], w["c_D"], w["c_norm"], batch, t_pad, tt, tv_last)
        h = _channel_mix(h, ya, unpad(yb), unpad(yc), p[l].reshape(n, -1), w["w_out"], w["g_ffn"], w["g_ple"],
                         w["w_ple_gate"], w["w_ple_proj"], tm, FF_TILE, ffn=w.get("ffn"), moe=w.get("moe"))
        for r, a in zip(rec, (k_rows, v_rows,
                              sm32[:, SM_KI:SM_KI + D_I].reshape(batch, t, D_I), bconv, bs_, cconv, cs_)):
            r.append(a)
    return h.reshape(batch, t, d), [jnp.stack(r) for r in rec]


def kernel(x_prompt, x_sample, cache_k, cache_v, cache_kidx, state_b_conv, state_b_ssm, state_c_conv, state_c_ssm, page_table, p_prompt, p_sample, w_in, w_out, g_mix, g_ffn, g_ple, q_norm, k_norm, b_conv_w, b_A_log, b_dt_bias, b_norm, c_conv_w, c_conv_b, c_A_log, c_dt_bias, c_D, c_norm, ffn_w1, ffn_w3, ffn_w2, moe_router, moe_w1, moe_w3, moe_w2, w_ple_gate, w_ple_proj):
    depth = w_in.shape[0]
    lw = []
    for l in range(depth):
        w = dict(w_in=_arrange_w_in(w_in[l]), w_out=w_out[l].astype(MM), g_mix=g_mix[l], g_ffn=g_ffn[l],
                 g_ple=g_ple[l], q_norm=q_norm[l], k_norm=k_norm[l], b_conv_w=b_conv_w[l], b_A_log=b_A_log[l],
                 b_dt_bias=b_dt_bias[l], b_norm=b_norm[l], c_conv_w=c_conv_w[l], c_conv_b=c_conv_b[l],
                 c_A_log=c_A_log[l], c_dt_bias=c_dt_bias[l], c_D=c_D[l], c_norm=c_norm[l],
                 w_ple_gate=w_ple_gate[l].astype(MM), w_ple_proj=w_ple_proj[l].astype(MM))
        j = l // 2
        if l % 2 == 0:
            w["ffn"] = (ffn_w1[j].astype(MM), ffn_w3[j].astype(MM), ffn_w2[j].astype(MM))
        else:
            w["moe"] = (moe_router[j], moe_w1[j].astype(MM), moe_w3[j].astype(MM), moe_w2[j].astype(MM))
        lw.append(w)

    bp, tp, _ = x_prompt.shape
    bs, ts, _ = x_sample.shape
    zeros = lambda *s: jnp.zeros((depth, bp) + s, F32)
    y_p, (k_p, v_p, ki_p, bc_p, bs_p, cc_p, cs_p) = _trunk(
        x_prompt, p_prompt, lw, zeros(CONV_W - 1, CB), zeros(H_B, DK_B, DV_B), zeros(CONV_W - 1, CC),
        zeros(H_C, P_C, N_C), None, min(512, bp * tp), min(256, tp))
    y_s, (k_s, v_s, ki_s, bc_s, bs_s, cc_s, cs_s) = _trunk(
        x_sample, p_sample, lw, state_b_conv, state_b_ssm, state_c_conv, state_c_ssm,
        (cache_k, cache_v, cache_kidx, page_table), bs * ts, CHUNK)
    return (y_p, y_s, k_p, k_s, v_p, v_s, ki_p, ki_s, bc_p, bc_s, bs_p, bs_s, cc_p, cc_s, cs_p, cs_s)
```

```python
import functools
import math

import numpy as np
import jax
import jax.numpy as jnp
from jax import lax
from jax.experimental import pallas as pl
from jax.experimental.pallas import tpu as pltpu

F32 = jnp.float32
I32 = jnp.int32
MM = jnp.bfloat16
EPS = 1e-6
NEG = -1e30
INT_MIN = -2 ** 31
LOG2E = 1.4426950408889634

LANES = 128
SUBLANES = 8
VMEM_LIMIT = 56 * 1024 * 1024

H_A, DH_A, H_I, D_I, TOPK, Q_BLOCK = 8, 64, 8, 64, 256, 128
H_B, DK_B, DV_B = 4, 64, 64
H_C, P_C, G_C, N_C = 4, 64, 2, 128
CONV_W, CHUNK = 4, 64
N_EXPERTS, TOP_E = 8, 2
FF_TILE = 512
W_A, W_B, W_C = H_A * DH_A, H_B * DV_B, H_C * P_C
CB = 2 * H_B * DK_B + H_B * DV_B
CC = W_C + 2 * G_C * N_C

SM_KI, SM_WI, SM_BETA, SM_A, SM_DT = 0, 64, 72, 76, 80


def _dot(a, b):
    return jnp.dot(a, b, preferred_element_type=F32)


def _dot_nt(a, b):
    return lax.dot_general(a, b, (((1,), (1,)), ((), ())), preferred_element_type=F32)


def _dot_tn(a, b):
    return lax.dot_general(a, b, (((0,), (0,)), ((), ())), preferred_element_type=F32)


def _split2(x):
    hi = x.astype(MM)
    lo = (x - hi.astype(F32)).astype(MM)
    return hi, lo


def _split3(x):
    a1 = x.astype(MM)
    r = x - a1.astype(F32)
    a2 = r.astype(MM)
    a3 = (r - a2.astype(F32)).astype(MM)
    return a1, a2, a3


def _dot_hi(a, b, f=_dot):
    ah, al = _split2(a)
    bh, bl = _split2(b)
    return f(ah, bh) + (f(ah, bl) + f(al, bh))


def _dot_x3(a, b_exact):
    a1, a2, a3 = _split3(a)
    return _dot(a1, b_exact) + (_dot(a2, b_exact) + _dot(a3, b_exact))


def _x3_dot(a_exact, b):
    b1, b2, b3 = _split3(b)
    return _dot(a_exact, b1) + (_dot(a_exact, b2) + _dot(a_exact, b3))


def _group_mean(xx, bd):
    hi, lo = _split2(xx)
    return _dot(hi, bd) + _dot(lo, bd)


def _sigmoid(x):
    return 1.0 / (1.0 + jnp.exp(-x))


def _silu(x):
    return x * _sigmoid(x)


def _softplus(x):
    return jnp.maximum(x, 0.0) + jnp.log(1.0 + jnp.exp(-jnp.abs(x)))


def _rms(x, g):
    ms = jnp.mean(x * x, axis=-1, keepdims=True)
    return x * lax.rsqrt(ms + EPS) * g


def _block_diag(width, group, value):
    i = np.arange(width)
    return jnp.asarray(np.where((i[:, None] // group) == (i[None, :] // group), value, 0.0), MM)


def _params(sem):
    return pltpu.CompilerParams(dimension_semantics=sem, vmem_limit_bytes=VMEM_LIMIT)


def _full(shape):
    n = len(shape)
    return pl.BlockSpec(shape, lambda *_: (0,) * n)


def _inproj_kernel(h_ref, g_ref, w_ref, qn_ref, kn_ref, bd_ref,
                   q16_ref, k32_ref, k16_ref, v32_ref, v16_ref, qi16_ref,
                   sm32_ref, sm16_ref, bqkv_ref, bz_ref, cz_ref, cxbc_ref, *smt_ref, feature_major):
    u = _rms(h_ref[...], g_ref[...]).astype(MM)

    def proj(a, b):
        return _dot_nt(u, w_ref[a:b, :])

    bd = bd_ref[...]
    aq = proj(0, W_A)
    q = aq * lax.rsqrt(_group_mean(aq * aq, bd) + EPS) * qn_ref[...]
    q16_ref[...] = (q * (DH_A ** -0.5 * LOG2E)).astype(MM)
    ak = proj(W_A, 2 * W_A)
    k = ak * lax.rsqrt(_group_mean(ak * ak, bd) + EPS) * kn_ref[...]
    k16_ref[...] = k.astype(MM)
    v = proj(2 * W_A, 3 * W_A)
    qi16_ref[...] = proj(3 * W_A, 4 * W_A).astype(MM)
    o = 4 * W_A
    sm = proj(o, o + LANES)
    sm32_ref[...] = sm
    sm16_ref[...] = sm.astype(MM)
    if feature_major:
        vt = v.T
        k32_ref[0] = k.T
        v32_ref[0] = vt
        v16_ref[0] = vt.astype(MM)
        smt_ref[0][...] = sm.T
    else:
        k32_ref[...] = k
        v32_ref[...] = v
        v16_ref[...] = v.astype(MM)
    o += LANES
    bqkv_ref[...] = proj(o, o + CB)
    bz_ref[...] = proj(o + CB, o + CB + W_B)
    o += CB + W_B
    cz_ref[...] = proj(o, o + W_C)
    cxbc_ref[...] = proj(o + W_C, o + W_C + CC)


def _arrange_w_in(w):
    w = w.T
    o_aik = 4 * W_A
    o_aiw = o_aik + D_I
    o_bqkv = o_aiw + H_I
    o_bz = o_bqkv + CB
    o_bbeta = o_bz + W_B
    o_ba = o_bbeta + H_B
    o_cz = o_ba + H_B
    o_cxbc = o_cz + W_C
    o_cdt = o_cxbc + CC
    small = jnp.concatenate(
        [w[o_aik:o_aiw], w[o_aiw:o_bqkv], w[o_bbeta:o_ba], w[o_ba:o_cz], w[o_cdt:o_cdt + H_C]], axis=0)
    small = jnp.pad(small, ((0, LANES - small.shape[0]), (0, 0)))
    return jnp.concatenate([w[:o_aik], small, w[o_bqkv:o_bbeta], w[o_cz:o_cdt]], axis=0).astype(MM)


def _inproj(h, g, w_arr, qn, kn, tm, feature_major_seq=None):
    n, d = h.shape
    wtot = w_arr.shape[0]
    bd = _block_diag(W_A, DH_A, 1.0 / DH_A)
    row = lambda w: pl.BlockSpec((tm, w), lambda i: (i, 0))
    rowout = lambda w, t: (row(w), jax.ShapeDtypeStruct((n, w), t))
    outs = [rowout(W_A, MM), rowout(W_A, F32), rowout(W_A, MM), rowout(W_A, F32), rowout(W_A, MM),
            rowout(W_A, MM), rowout(LANES, F32), rowout(LANES, MM), rowout(CB, F32), rowout(W_B, F32),
            rowout(W_C, F32), rowout(CC, F32)]
    if feature_major_seq is not None:
        tpb = feature_major_seq // tm
        per_seq = (pl.BlockSpec((1, W_A, tm), lambda i: (i // tpb, 0, i % tpb)),
                   jax.ShapeDtypeStruct((n // feature_major_seq, W_A, feature_major_seq), F32))
        outs[1] = per_seq
        outs[3] = per_seq
        outs[4] = (pl.BlockSpec((1, W_A, tm), lambda i: (i, 0, 0)),
                   jax.ShapeDtypeStruct((n // tm, W_A, tm), MM))
        outs.append((pl.BlockSpec((LANES, tm), lambda i: (0, i)), jax.ShapeDtypeStruct((LANES, n), F32)))
    return pl.pallas_call(
        functools.partial(_inproj_kernel, feature_major=feature_major_seq is not None),
        grid=(n // tm,),
        in_specs=[row(d), _full((1, d)), _full((wtot, d)), _full((1, W_A)), _full((1, W_A)),
                  _full((W_A, W_A))],
        out_specs=[s for s, _ in outs],
        out_shape=[t for _, t in outs],
        compiler_params=_params(("parallel",)),
        name="inproj",
    )(h, g.reshape(1, d), w_arr, jnp.tile(qn, H_A).reshape(1, W_A), jnp.tile(kn, H_A).reshape(1, W_A), bd)


def _mix_prologue(h_ref, ya_ref, yb_ref, yc_ref, wo_ref, gf_ref, h1_ref, u2_ref, acc_ref):
    h1 = h_ref[...] + (_dot(ya_ref[...], wo_ref[0:W_A, :])
                       + _dot(yb_ref[...], wo_ref[W_A:W_A + W_B, :])
                       + _dot(yc_ref[...], wo_ref[W_A + W_B:, :]))
    h1_ref[...] = h1
    u2 = _rms(h1, gf_ref[...])
    u2_ref[...] = u2.astype(MM)
    acc_ref[...] = jnp.zeros_like(acc_ref)
    return u2


def _mix_epilogue(p_ref, gp_ref, wg_ref, wp_ref, h1_ref, acc_ref, o_ref):
    h2 = h1_ref[...] + acc_ref[...]
    u3 = _rms(h2, gp_ref[...]).astype(MM)
    gate = _sigmoid(_dot(u3, wg_ref[...]))
    o_ref[...] = h2 + gate * _dot(p_ref[...].astype(MM), wp_ref[...])


def _ffn_kernel(h_ref, ya_ref, yb_ref, yc_ref, p_ref, wo_ref, gf_ref, w1_ref, w3_ref, w2_ref,
                gp_ref, wg_ref, wp_ref, o_ref, h1_ref, u2_ref, acc_ref):
    f = pl.program_id(1)

    @pl.when(f == 0)
    def _():
        _mix_prologue(h_ref, ya_ref, yb_ref, yc_ref, wo_ref, gf_ref, h1_ref, u2_ref, acc_ref)

    u2 = u2_ref[...]
    hid = _silu(_dot(u2, w1_ref[...])) * _dot(u2, w3_ref[...])
    acc_ref[...] += _dot(hid.astype(MM), w2_ref[...])

    @pl.when(f == pl.num_programs(1) - 1)
    def _():
        _mix_epilogue(p_ref, gp_ref, wg_ref, wp_ref, h1_ref, acc_ref, o_ref)


def _moe_kernel(h_ref, ya_ref, yb_ref, yc_ref, p_ref, wo_ref, gf_ref, wr_ref, w1_ref, w3_ref, w2_ref,
                gp_ref, wg_ref, wp_ref, o_ref, h1_ref, u2_ref, acc_ref, comb_ref):
    e = pl.program_id(1)
    f = pl.program_id(2)

    @pl.when((e == 0) & (f == 0))
    def _():
        u2 = _mix_prologue(h_ref, ya_ref, yb_ref, yc_ref, wo_ref, gf_ref, h1_ref, u2_ref, acc_ref)
        logits = _dot_hi(u2, wr_ref[...])
        lane = lax.broadcasted_iota(I32, logits.shape, 1)
        big = jnp.int32(LANES)
        lg = jnp.where(lane < N_EXPERTS, logits, -jnp.inf)
        m1 = jnp.max(lg, axis=1, keepdims=True)
        i1 = jnp.min(jnp.where(lg == m1, lane, big), axis=1, keepdims=True)
        lg2 = jnp.where(lane == i1, -jnp.inf, lg)
        m2 = jnp.max(lg2, axis=1, keepdims=True)
        i2 = jnp.min(jnp.where(lg2 == m2, lane, big), axis=1, keepdims=True)
        e2 = jnp.exp(m2 - m1)
        g1 = 1.0 / (1.0 + e2)
        g2 = e2 / (1.0 + e2)
        for ex in range(N_EXPERTS):
            ce = jnp.where(i1 == ex, g1, 0.0) + jnp.where(i2 == ex, g2, 0.0)
            comb_ref[ex] = jnp.broadcast_to(ce, comb_ref.shape[1:])

    u2 = u2_ref[...]
    hid = _silu(_dot(u2, w1_ref[0])) * _dot(u2, w3_ref[0])
    tf = hid.shape[1]
    ce = comb_ref[e]
    hid = hid * jnp.concatenate([ce] * (tf // LANES), axis=1)
    acc_ref[...] += _dot(hid.astype(MM), w2_ref[0])

    @pl.when((e == pl.num_programs(1) - 1) & (f == pl.num_programs(2) - 1))
    def _():
        _mix_epilogue(p_ref, gp_ref, wg_ref, wp_ref, h1_ref, acc_ref, o_ref)


def _channel_mix(h, ya, yb, yc, p, wo, gf, gp, wg, wp, tm, tf, *, ffn=None, moe=None):
    n, d = h.shape
    dple = p.shape[1]
    if ffn is not None:
        w1, w3, w2 = ffn
        dff = w1.shape[1]
        grid = (n // tm, dff // tf)
        row = lambda w: pl.BlockSpec((tm, w), lambda i, f: (i, 0))
        wspecs = [pl.BlockSpec((d, tf), lambda i, f: (0, f)), pl.BlockSpec((d, tf), lambda i, f: (0, f)),
                  pl.BlockSpec((tf, d), lambda i, f: (f, 0))]
        wargs = [w1, w3, w2]
        body = _ffn_kernel
        extra_scratch = []
        sem = ("parallel", "arbitrary")
    else:
        wr, w1, w3, w2 = moe
        ne, _, dff = w1.shape
        grid = (n // tm, ne, dff // tf)
        row = lambda w: pl.BlockSpec((tm, w), lambda i, e, f: (i, 0))
        wspecs = [_full((d, LANES)),
                  pl.BlockSpec((1, d, tf), lambda i, e, f: (e, 0, f)),
                  pl.BlockSpec((1, d, tf), lambda i, e, f: (e, 0, f)),
                  pl.BlockSpec((1, tf, d), lambda i, e, f: (e, f, 0))]
        wargs = [jnp.pad(wr, ((0, 0), (0, LANES - ne))), w1, w3, w2]
        body = _moe_kernel
        extra_scratch = [pltpu.VMEM((ne, tm, LANES), F32)]
        sem = ("parallel", "arbitrary", "arbitrary")
    return pl.pallas_call(
        body,
        grid=grid,
        in_specs=[row(d), row(W_A), row(W_B), row(W_C), row(dple), _full((d, d)), _full((1, d))]
                 + wspecs + [_full((1, d)), _full((d, d)), _full((dple, d))],
        out_specs=row(d),
        out_shape=jax.ShapeDtypeStruct((n, d), F32),
        scratch_shapes=[pltpu.VMEM((tm, d), F32), pltpu.VMEM((tm, d), MM), pltpu.VMEM((tm, d), F32)]
                       + extra_scratch,
        compiler_params=_params(sem),
        name="ffn" if ffn is not None else "moe",
    )(h, ya, yb, yc, p, wo, gf.reshape(1, d), *wargs, gp.reshape(1, d), wg, wp)


def _sortable(x):
    b = lax.bitcast_convert_type(x, I32)
    return b ^ ((b >> 31) & jnp.int32(0x7FFFFFFF))


def _select_topk(keys_ref, bias_ref, n_groups, group, topk, idx_bits, tile_shape, key_axis, half_ref=None):
    imin = jnp.int32(INT_MIN)
    pos = lax.broadcasted_iota(I32, tile_shape, key_axis)
    if key_axis == 1:
        qshape = tile_shape
        full = lambda x: x
        acc0 = jnp.zeros(tile_shape, F32)
        add_tile = lambda acc, m: acc + m
        finish = lambda acc: jnp.broadcast_to(jnp.sum(acc, axis=1, keepdims=True), tile_shape)
    else:
        qshape = (1, tile_shape[1])
        full = lambda x: jnp.broadcast_to(x, tile_shape)
        acc0 = jnp.zeros((SUBLANES, tile_shape[1]), F32)

        def add_tile(acc, m):
            parts = [m[s * SUBLANES:(s + 1) * SUBLANES] for s in range(tile_shape[0] // SUBLANES)]
            while len(parts) > 1:
                parts = [a + b for a, b in zip(parts[::2], parts[1::2])]
            return acc + parts[0]

        finish = lambda acc: jnp.sum(acc, axis=0, keepdims=True)

    def count(pred):
        def body(i, acc):
            for t in range(group):
                j = i * group + t
                acc = add_tile(acc, jnp.where(pred(keys_ref[j], j), 1.0, 0.0))
            return acc
        return finish(lax.fori_loop(0, n_groups, body, acc0))

    def value_bit(b, carry):
        thr, cnt_thr = carry
        cand = thr + jnp.left_shift(jnp.int32(1), 31 - b)
        cand_f = full(cand)
        cnt = count(lambda k, j: k >= cand_f)
        keep = cnt >= topk
        return jnp.where(keep, cand, thr), jnp.where(keep, cnt, cnt_thr)

    if half_ref is None:
        thr, cnt_thr = lax.fori_loop(0, 32, value_bit,
                                     (jnp.full(qshape, imin), jnp.full(qshape, 2.0 ** 30, F32)))
    else:
        assert key_axis == 0
        half_rows = 2 * SUBLANES
        lo_min = jnp.int32(-2 ** 15)

        def count16(pred):
            def body(i, acc):
                for t in range(group):
                    m = jnp.where(pred(half_ref[i * group + t]), jnp.int16(1), jnp.int16(0))
                    parts = [m[s * half_rows:(s + 1) * half_rows] for s in range(tile_shape[0] // half_rows)]
                    while len(parts) > 1:
                        parts = [a + b for a, b in zip(parts[::2], parts[1::2])]
                    acc = acc + parts[0]
                return acc
            acc = lax.fori_loop(0, n_groups, body, jnp.zeros((half_rows, tile_shape[1]), jnp.int16))
            return jnp.sum(acc.astype(F32), axis=0, keepdims=True)

        def fill_halves(fn):
            def body(i, carry):
                for t in range(group):
                    half_ref[i * group + t] = fn(keys_ref[i * group + t]).astype(jnp.int16)
                return carry
            lax.fori_loop(0, n_groups, body, 0)

        def search16(target):
            def bit(b, t):
                cand = t + jnp.left_shift(jnp.int32(1), 15 - b)
                cand_h = full(cand).astype(jnp.int16)
                return jnp.where(count16(lambda h: h >= cand_h) >= target, cand, t)
            return lax.fori_loop(0, 16, bit, jnp.full(qshape, lo_min))

        fill_halves(lambda k: k >> 16)
        hi = search16(topk)
        hi_h = full(hi).astype(jnp.int16)
        need_lo = topk - count16(lambda h: h > hi_h)
        hi_f = full(hi)
        fill_halves(lambda k: jnp.where((k >> 16) == hi_f, (k & 0xFFFF) + lo_min, lo_min))
        lo = search16(need_lo)
        thr = hi * 65536 + (lo - lo_min)
        thr_f = full(thr)
        cnt_thr = count(lambda k, j: k >= thr_f)
    no_cut = jnp.full(qshape, 2 ** idx_bits - 1, I32)
    tied = (cnt_thr > topk) & (thr > imin)
    floor = jnp.maximum(thr, imin + 1)
    floor_f = full(floor)

    def index_cut():
        need = topk - count(lambda k, j: k > floor_f)

        def index_bit(b, c):
            cc = c | jnp.left_shift(jnp.int32(1), idx_bits - 1 - b)
            cc_f = full(cc)
            cnt = count(lambda k, j: (k == floor_f) & (j * LANES + pos < cc_f))
            return jnp.where(cnt <= need, cc, c)

        return lax.fori_loop(0, idx_bits, index_bit, jnp.zeros(qshape, I32))

    cut = lax.cond(jnp.max(jnp.where(tied, 1.0, 0.0)) > 0.0, index_cut, lambda: no_cut)
    cut_f = full(jnp.where(tied, cut, no_cut))

    def write(i, carry):
        for t in range(group):
            j = i * group + t
            k = keys_ref[j]
            sel = (k > floor_f) | ((k == floor_f) & (j * LANES + pos < cut_f))
            bias_ref[j] = jnp.where(sel, 0.0, NEG)
        return carry

    lax.fori_loop(0, n_groups, write, 0)


def _two_buffer_pipeline(n, products, finish, back, bufs):
    a, b = bufs
    finish(0, products(0), a)

    def step(c, cur, nxt):
        @pl.when(c + 1 < n)
        def _():
            back(c, cur, lambda: products(c + 1), lambda vals: finish(c + 1, vals, nxt))

        @pl.when(c + 1 == n)
        def _():
            back(c, cur, lambda: None, lambda vals: None)

    def body(k, carry):
        step(2 * k, a, b)
        step(2 * k + 1, b, a)
        return carry

    lax.fori_loop(0, (n + 1) // 2, body, 0)


def _dsa_prompt_kernel(q_ref, qi_ref, smt_ref, k_ref, vt_ref, ki_ref, o_ref,
                       keys_ref, bias_ref, half_ref, qp_ref, qip_ref, m_ref, mb_ref, l_ref, acc_ref,
                       sta_ref, stb_ref, *, topk, kc, idx_bits):
    i = pl.program_id(1)
    qb = q_ref.shape[0]
    tpc = kc // LANES
    n_chunks = ((i + 1) * qb + kc - 1) // kc
    n_pairs = H_A // 2
    tile = (LANES, qb)
    key_pos = lax.broadcasted_iota(I32, tile, 0)
    q_pos = i * qb + lax.broadcasted_iota(I32, tile, 1)
    low = lax.broadcasted_iota(I32, (qb, LANES), 1) < DH_A

    for h in range(H_A):
        pr, half = h // 2, h % 2
        rows = slice(half * qb, (half + 1) * qb)
        blk = q_ref[:, pr * LANES:(pr + 1) * LANES]
        qp_ref[pr, rows, :] = jnp.where(low if half == 0 else ~low, blk, jnp.zeros_like(blk))
        blk = qi_ref[:, pr * LANES:(pr + 1) * LANES]
        if half == 1:
            blk = pltpu.roll(blk, DH_A, 1)
        qip_ref[pr, rows, :] = jnp.where(low, blk, jnp.zeros_like(blk))
    wt = smt_ref[SM_WI:SM_WI + H_I, :] * ((D_I ** -0.5) * (H_I ** -0.5))

    def score_chunk(c, carry):
        base = pl.multiple_of(c * kc, kc)
        kt = ki_ref[pl.ds(base, kc), :]
        ss = [_dot_nt(kt, qip_ref[pr]) for pr in range(n_pairs)]
        for t in range(tpc):
            sc = jnp.zeros(tile, F32)
            for h in range(H_I):
                blk = ss[h // 2][t * LANES:(t + 1) * LANES, (h % 2) * qb:(h % 2 + 1) * qb]
                sc = sc + wt[h:h + 1, :] * jnp.maximum(blk, 0.0)
            valid = base + t * LANES + key_pos <= q_pos
            keys_ref[c * tpc + t] = jnp.where(valid, _sortable(sc), jnp.int32(INT_MIN))
        return carry

    lax.fori_loop(0, n_chunks, score_chunk, 0)
    _select_topk(keys_ref, bias_ref, n_chunks, tpc, topk, idx_bits, tile, 0, half_ref)

    m_ref[...] = jnp.full(m_ref.shape, NEG, F32)
    mb_ref[...] = jnp.full(mb_ref.shape, NEG, F32)
    l_ref[...] = jnp.zeros(l_ref.shape, F32)
    acc_ref[...] = jnp.zeros(acc_ref.shape, F32)

    def products(c):
        base = pl.multiple_of(c * kc, kc)
        return [_dot_nt(k_ref[pl.ds(base, kc), pr * LANES:(pr + 1) * LANES], qp_ref[pr])
                for pr in range(n_pairs)]

    def front(c, prods, buf):
        for pr in range(n_pairs):
            st = []
            for t in range(tpc):
                b = bias_ref[c * tpc + t]
                x = prods[pr][t * LANES:(t + 1) * LANES, :] + jnp.concatenate([b, b], axis=1)
                buf[pr, t * LANES:(t + 1) * LANES, :] = x
                st.append(x)
            cmax = jnp.max(functools.reduce(jnp.maximum, st), axis=0, keepdims=True)
            m_ref[pr] = jnp.maximum(m_ref[pr], cmax)

    def back(c, buf, start_next, finish_next):
        m_now = [m_ref[pr] for pr in range(n_pairs)]
        nxt = start_next()
        for pr in range(n_pairs):
            alpha = jnp.exp2(mb_ref[pr] - m_now[pr])
            ps = [jnp.exp2(buf[pr, t * LANES:(t + 1) * LANES, :] - m_now[pr]) for t in range(tpc)]
            l_ref[pr] = l_ref[pr] * alpha + jnp.sum(functools.reduce(jnp.add, ps), axis=0, keepdims=True)
            p = jnp.concatenate(ps, axis=0).astype(MM)
            acc_ref[pr] = acc_ref[pr] * alpha + _dot(vt_ref[c, pr * LANES:(pr + 1) * LANES, :], p)
            mb_ref[pr] = m_now[pr]
        finish_next(nxt)

    _two_buffer_pipeline(n_chunks, products, front, back, (sta_ref, stb_ref))
    for pr in range(n_pairs):
        o = acc_ref[pr] / l_ref[pr]
        ot = jnp.concatenate([o[:DH_A, :qb], o[DH_A:, qb:]], axis=0)
        o_ref[:, pr * LANES:(pr + 1) * LANES] = ot.T.astype(o_ref.dtype)


def _dsa_prompt(q16, qi16, smt, k16, vt, sm16, batch, seq, kc):
    qb = Q_BLOCK
    topk = min(TOPK, seq // 4)
    nq = seq // qb
    idx_bits = int(seq).bit_length() + 1
    assert H_A == H_I and DH_A == D_I and qb == LANES
    n_kc = seq // kc
    n_pairs = H_A // 2
    blk = lambda w: pl.BlockSpec((qb, w), lambda b, i: (b * nq + i, 0))
    seqblk = lambda w: pl.BlockSpec((seq, w), lambda b, i: (b, 0), pipeline_mode=pl.Buffered(1))
    tiles = (seq // LANES, LANES, qb)
    return pl.pallas_call(
        functools.partial(_dsa_prompt_kernel, topk=topk, kc=kc, idx_bits=idx_bits),
        grid=(batch, nq),
        in_specs=[blk(W_A), blk(W_A), pl.BlockSpec((LANES, qb), lambda b, i: (0, b * nq + i)), seqblk(W_A),
                  pl.BlockSpec((n_kc, W_A, kc), lambda b, i: (b, 0, 0), pipeline_mode=pl.Buffered(1)),
                  seqblk(LANES)],
        out_specs=blk(W_A),
        out_shape=jax.ShapeDtypeStruct((batch * seq, W_A), MM),
        scratch_shapes=[pltpu.VMEM(tiles, I32), pltpu.VMEM(tiles, F32), pltpu.VMEM(tiles, jnp.int16),
                        pltpu.VMEM((n_pairs, 2 * qb, LANES), MM), pltpu.VMEM((n_pairs, 2 * qb, LANES), MM),
                        pltpu.VMEM((n_pairs, 1, 2 * qb), F32), pltpu.VMEM((n_pairs, 1, 2 * qb), F32),
                        pltpu.VMEM((n_pairs, 1, 2 * qb), F32), pltpu.VMEM((n_pairs, LANES, 2 * qb), F32),
                        pltpu.VMEM((n_pairs, kc, 2 * qb), F32), pltpu.VMEM((n_pairs, kc, 2 * qb), F32)],
        compiler_params=_params(("parallel", "arbitrary")),
        name="dsa_prompt",
    )(q16, qi16, smt, k16, vt, sm16)


def _expand_matrix(first_lane, heads, width):
    m = np.zeros((LANES, heads * width), np.float32)
    for h in range(heads):
        m[first_lane + h, h * width:(h + 1) * width] = 1.0
    return jnp.asarray(m, MM)


def _chunk_masks(tt, chunk):
    i = np.arange(tt)
    same = (i[:, None] // chunk) == (i[None, :] // chunk)
    ltri = jnp.asarray(np.where(same & (i[:, None] >= i[None, :]), 1.0, 0.0), MM)
    utri = jnp.asarray(np.where(same & (i[:, None] <= i[None, :]), 1.0, 0.0), F32)
    return ltri, utri


def _lane_vec(values, first_lane):
    return jnp.zeros((1, LANES), F32).at[0, first_lane:first_lane + values.shape[0]].set(values)


def _conv_tile(x_ref, buf_ref, cw_ref, xe_ref, nb_ref, t, n_t, tv_last):
    tt = x_ref.shape[0]
    hist = SUBLANES

    @pl.when(t == 0)
    def _():
        xe_ref[0:hist, :] = jnp.zeros((hist, xe_ref.shape[1]), F32)
        xe_ref[hist - (CONV_W - 1):hist, :] = buf_ref[0]

    xe_ref[hist:hist + tt, :] = x_ref[...]
    y = jnp.zeros(x_ref.shape, F32)
    for i in range(CONV_W):
        o = hist - (CONV_W - 1) + i
        y = y + xe_ref[o:o + tt, :] * cw_ref[i:i + 1, :]

    @pl.when(t == n_t - 1)
    def _():
        nb_ref[0] = xe_ref[hist + tv_last - (CONV_W - 1):hist + tv_last, :]

    xe_ref[0:hist, :] = xe_ref[tt:tt + hist, :]
    return y


def _live_rows(shape, t, n_t, tv_last):
    row = lax.broadcasted_iota(I32, shape, 0)
    return (t < n_t - 1) | (row < tv_last)


def _row_cumsum(vals, utri, ones_c, heads, width, chunk):
    tt = vals.shape[0]
    out = []
    for h in range(heads):
        col = vals[:, h * width:h * width + chunk]
        out.append(_x3_dot(ones_c, jnp.concatenate([col] * (tt // chunk), axis=1) * utri))
    return out


def _gdn_kernel(x_ref, z_ref, sm_ref, buf_ref, s0_ref, cw_ref, alog_ref, dtb_ref, ng_ref,
                ebeta_ref, eg_ref, ltri_ref, utri_ref, bd1_ref, bdn_ref,
                y_ref, nb_ref, st_ref, xe_ref, s_ref, o_ref, *, tv_last, chunk):
    t = pl.program_id(1)
    n_t = pl.num_programs(1)
    tt = x_ref.shape[0]
    wq = H_B * DK_B

    @pl.when(t == 0)
    def _():
        s_ref[...] = s0_ref[0]

    qkv = _silu(_conv_tile(x_ref, buf_ref, cw_ref, xe_ref, nb_ref, t, n_t, tv_last))
    qk = qkv[:, :2 * wq]
    qk = qk * lax.rsqrt(_group_mean(qk * qk, bd1_ref[...]) + EPS)
    q = qk[:, :wq] * (DK_B ** -0.5)
    k = qk[:, wq:]
    v = qkv[:, 2 * wq:]

    sm = sm_ref[...]
    beta = _sigmoid(sm)
    g = -jnp.exp(alog_ref[...]) * _softplus(sm + dtb_ref[...])
    if tv_last < tt:
        live = _live_rows(sm.shape, t, n_t, tv_last)
        beta = jnp.where(live, beta, 0.0)
        g = jnp.where(live, g, 0.0)
    betab = _dot_x3(beta, ebeta_ref[...])
    gb = _dot_x3(g, eg_ref[...])
    gc = _x3_dot(ltri_ref[...], gb)
    ones_c = jnp.ones((chunk, tt), MM)
    gr = _row_cumsum(gb, utri_ref[...], ones_c, H_B, DK_B, chunk)
    eg = jnp.exp(gc)
    kb = k * betab
    vb = v * betab
    kbeg = kb * eg
    qg = q * eg
    ii = lax.broadcasted_iota(I32, (chunk, chunk), 0)
    jj = lax.broadcasted_iota(I32, (chunk, chunk), 1)
    levels = max(1, (chunk - 1).bit_length())

    n_c = tt // chunk
    units = [(c, h) for c in range(n_c) for h in range(H_B)]
    rows = lambda c: slice(c * chunk, (c + 1) * chunk)
    cols = lambda h: slice(h * DK_B, (h + 1) * DK_B)
    k16 = k.astype(MM)
    kk = {u: _dot_nt(kb[rows(u[0]), cols(u[1])].astype(MM), k16[rows(u[0]), cols(u[1])]) for u in units}
    qk = {u: _dot_nt(q[rows(u[0]), cols(u[1])].astype(MM), k16[rows(u[0]), cols(u[1])]) for u in units}
    xs, ps, attn = {}, {}, {}
    for u in units:
        c, h = u
        decay = jnp.exp(jnp.where(ii >= jj, gc[rows(c), cols(h)] - gr[h][:, rows(c)], NEG))
        ps[u] = -jnp.where(ii > jj, kk[u] * decay, 0.0)
        attn[u] = (qk[u] * decay).astype(MM)
        xs[u] = jnp.concatenate([vb[rows(c), cols(h)], kbeg[rows(c), cols(h)]], axis=1)
    wx = 2 * DV_B
    for lv in range(levels):
        last_lv = lv == levels - 1
        prod = {u: _dot_hi(ps[u], xs[u] if last_lv else jnp.concatenate([xs[u], ps[u]], axis=1)) for u in units}
        for u in units:
            xs[u] = xs[u] + prod[u][:, :wx]
            if not last_lv:
                ps[u] = prod[u][:, wx:]

    for c in range(n_c):
        last = gc[(c + 1) * chunk - 1:(c + 1) * chunk, :]
        kg = (k[rows(c), :] * jnp.exp(last - gc[rows(c), :])).astype(MM)
        gl = jnp.exp(last)
        qg16 = qg[rows(c), :].astype(MM)
        s_old = [s_ref[h] for h in range(H_B)]
        s16 = [s.astype(MM) for s in s_old]
        ws = [_dot(xs[c, h][:, DV_B:].astype(MM), s16[h]) for h in range(H_B)]
        qs = [_dot(qg16[:, cols(h)], s16[h]) for h in range(H_B)]
        u16 = [(xs[c, h][:, :DV_B] - ws[h]).astype(MM) for h in range(H_B)]
        au = [_dot(attn[c, h], u16[h]) for h in range(H_B)]
        ku = [_dot_tn(kg[:, cols(h)], u16[h]) for h in range(H_B)]
        for h in range(H_B):
            o_ref[rows(c), cols(h)] = qs[h] + au[h]
            s_ref[h] = s_old[h] * gl[:, h * DK_B:h * DK_B + 1] + ku[h]

    o = o_ref[...]
    o = o * lax.rsqrt(_group_mean(o * o, bdn_ref[...]) + EPS) * ng_ref[...]
    y_ref[...] = (o * _silu(z_ref[...])).astype(y_ref.dtype)

    @pl.when(t == n_t - 1)
    def _():
        st_ref[0] = s_ref[...]


def _ssd_kernel(x_ref, z_ref, sm_ref, buf_ref, h0_ref, cw_ref, cb_ref, alog_ref, dtb_ref, d_ref, ng_ref,
                edt_ref, ltri_ref, utri_ref, bdn_ref,
                y_ref, nb_ref, ht_ref, xe_ref, h_ref, o_ref, *, tv_last, chunk):
    t = pl.program_id(1)
    n_t = pl.num_programs(1)
    tt = x_ref.shape[0]
    gw = G_C * N_C
    hpg = H_C // G_C

    @pl.when(t == 0)
    def _():
        h_ref[...] = h0_ref[0]

    xbc = _silu(_conv_tile(x_ref, buf_ref, cw_ref, xe_ref, nb_ref, t, n_t, tv_last) + cb_ref[...])
    xs = xbc[:, :W_C]
    bm = xbc[:, W_C:W_C + gw].astype(MM)
    cm = xbc[:, W_C + gw:].astype(MM)

    dt = _softplus(sm_ref[...] + dtb_ref[...])
    if tv_last < tt:
        dt = jnp.where(_live_rows(dt.shape, t, n_t, tv_last), dt, 0.0)
    dtb = _dot_x3(dt, edt_ref[...])
    a = dtb * (-jnp.exp(alog_ref[...]))
    xdt = xs * dtb
    ac = _x3_dot(ltri_ref[...], a)
    ones_c = jnp.ones((chunk, tt), MM)
    ar = _row_cumsum(a, utri_ref[...], ones_c, H_C, P_C, chunk)
    ea = jnp.exp(ac)
    ii = lax.broadcasted_iota(I32, (chunk, chunk), 0)
    jj = lax.broadcasted_iota(I32, (chunk, chunk), 1)

    for c in range(tt // chunk):
        rs = slice(c * chunk, (c + 1) * chunk)
        last = ac[(c + 1) * chunk - 1:(c + 1) * chunk, :]
        xdec = (xdt[rs, :] * jnp.exp(last - ac[rs, :])).astype(MM)
        hdec = jnp.exp(last)
        for g in range(G_C):
            gs = slice(g * N_C, (g + 1) * N_C)
            cbm = _dot_nt(cm[rs, gs], bm[rs, gs])
            for h in range(g * hpg, (g + 1) * hpg):
                hs = slice(h * P_C, (h + 1) * P_C)
                lmat = jnp.exp(jnp.where(ii >= jj, ac[rs, hs] - ar[h][:, rs], NEG))
                hprev = h_ref[h]
                y = _dot((cbm * lmat).astype(MM), xdt[rs, hs].astype(MM))
                y = y + ea[rs, hs] * _dot_nt(cm[rs, gs], hprev.astype(MM))
                o_ref[rs, hs] = y + d_ref[:, hs] * xs[rs, hs]
                h_ref[h] = hprev * hdec[:, h * P_C:h * P_C + 1] + _dot_tn(xdec[:, hs], bm[rs, gs])

    yz = o_ref[...] * _silu(z_ref[...])
    y_ref[...] = (yz * lax.rsqrt(_group_mean(yz * yz, bdn_ref[...]) + EPS) * ng_ref[...]).astype(y_ref.dtype)

    @pl.when(t == n_t - 1)
    def _():
        ht_ref[0] = h_ref[...]


def _recurrent_call(body, name, x, z, sm, buf, state0, vecs, consts, batch, t_pad, tt, tv_last, width_out):
    n_t = t_pad // tt
    cw = x.shape[1]
    blk = lambda w: pl.BlockSpec((tt, w), lambda b, t: (b * n_t + t, 0))
    per_b = lambda shp: pl.BlockSpec((1,) + shp, lambda b, t: (b,) + (0,) * len(shp))
    sshape = state0.shape[1:]
    return pl.pallas_call(
        functools.partial(body, tv_last=tv_last, chunk=min(CHUNK, tt)),
        grid=(batch, n_t),
        in_specs=[blk(cw), blk(width_out), blk(LANES), per_b((CONV_W - 1, cw)), per_b(sshape)]
                 + [_full(v.shape) for v in vecs] + [_full(c.shape) for c in consts],
        out_specs=[blk(width_out), per_b((CONV_W - 1, cw)), per_b(sshape)],
        out_shape=[jax.ShapeDtypeStruct((batch * t_pad, width_out), MM),
                   jax.ShapeDtypeStruct((batch, CONV_W - 1, cw), F32),
                   jax.ShapeDtypeStruct((batch,) + sshape, F32)],
        scratch_shapes=[pltpu.VMEM((tt + SUBLANES, cw), F32), pltpu.VMEM(sshape, F32),
                        pltpu.VMEM((tt, width_out), F32)],
        compiler_params=_params(("parallel", "arbitrary")),
        name=name,
    )(x, z, sm, buf, state0, *vecs, *consts)


def _gdn(x, z, sm, buf, s0, conv_w, a_log, dt_bias, norm_g, batch, t_pad, tt, tv_last):
    ltri, utri = _chunk_masks(tt, min(CHUNK, tt))
    vecs = [conv_w, _lane_vec(a_log, SM_A), _lane_vec(dt_bias, SM_A), jnp.tile(norm_g, H_B).reshape(1, W_B)]
    consts = [_expand_matrix(SM_BETA, H_B, DK_B), _expand_matrix(SM_A, H_B, DK_B), ltri, utri,
              _block_diag(2 * H_B * DK_B, DK_B, 1.0), _block_diag(W_B, DV_B, 1.0 / DV_B)]
    return _recurrent_call(_gdn_kernel, "gdn", x, z, sm, buf, s0, vecs, consts, batch, t_pad, tt, tv_last, W_B)


def _ssd(x, z, sm, buf, h0, conv_w, conv_b, a_log, dt_bias, dskip, norm_g, batch, t_pad, tt, tv_last):
    ltri, utri = _chunk_masks(tt, min(CHUNK, tt))
    vecs = [conv_w, conv_b.reshape(1, CC), jnp.repeat(a_log, P_C).reshape(1, W_C), _lane_vec(dt_bias, SM_DT),
            jnp.repeat(dskip, P_C).reshape(1, W_C), norm_g.reshape(1, W_C)]
    consts = [_expand_matrix(SM_DT, H_C, P_C), ltri, utri, _block_diag(W_C, W_C // G_C, float(G_C) / W_C)]
    return _recurrent_call(_ssd_kernel, "ssd", x, z, sm, buf, h0, vecs, consts, batch, t_pad, tt, tv_last, W_C)


def _dsa_sample_select_kernel(pt_ref, qi_ref, w_ref, kin_ref, *rest, topk, n_q, idx_bits, pg, group):
    pages = rest[:pg]
    bias_ref, keys_ref = rest[pg:]
    j = pl.program_id(1)
    n_s = pl.num_programs(1)
    n_tiles = bias_ref.shape[1]
    shape = (SUBLANES, LANES)
    row = lax.broadcasted_iota(I32, shape, 0)
    lane = lax.broadcasted_iota(I32, shape, 1)
    imin = jnp.int32(INT_MIN)

    def keys_of(s, valid):
        r = jnp.maximum(s, 0.0) * w_ref[0]
        per_q = [jnp.sum(r[q * H_I:(q + 1) * H_I], axis=0, keepdims=True) for q in range(n_q)]
        sc = jnp.concatenate(per_q + [jnp.zeros((SUBLANES - n_q, LANES), F32)], axis=0)
        return jnp.where(valid, _sortable(sc), imin)

    qi = qi_ref[0]
    ss = [_dot(qi, pages[t][0, 0].astype(MM)) for t in range(pg)]
    for t in range(pg):
        keys_ref[j * pg + t] = keys_of(ss[t], row < n_q)

    @pl.when(j == n_s - 1)
    def _():
        keys_ref[n_tiles - 1] = keys_of(_dot(qi, kin_ref[0]), (row < n_q) & (lane <= row))
        _select_topk(keys_ref, bias_ref.at[0], n_tiles // group, group, topk, idx_bits, shape, 1)


def _dsa_sample_attend_kernel(pt_ref, q_ref, bias_ref, bnew_ref, kn_ref, vn_ref, *rest, pg):
    kps, vps = rest[:pg], rest[pg:2 * pg]
    o_ref, qbd_ref, m_ref, l_ref, acc_ref = rest[2 * pg:]
    j = pl.program_id(1)
    n_s = pl.num_programs(1)
    rows = H_A * SUBLANES
    shape = (rows, LANES)
    head_of_row = lax.broadcasted_iota(I32, (rows, W_A), 0) // SUBLANES
    head_of_lane = lax.broadcasted_iota(I32, (rows, W_A), 1) // DH_A
    own = head_of_row == head_of_lane

    @pl.when(j == 0)
    def _():
        qrep = jnp.concatenate([q_ref[0]] * H_A, axis=0)
        qbd_ref[...] = jnp.where(own, qrep, jnp.zeros_like(qrep))
        m_ref[...] = jnp.full(shape, NEG, F32)
        l_ref[...] = jnp.zeros(shape, F32)
        acc_ref[...] = jnp.zeros((rows, W_A), F32)

    def absorb(kks, vvs, biases):
        qbd = qbd_ref[...]
        ss = [_dot(qbd, kk) for kk in kks]
        bs = [jnp.concatenate([b] * H_A, axis=0) for b in biases]
        ss = [s + b for s, b in zip(ss, bs)]
        m_old = m_ref[...]
        mx = jnp.max(functools.reduce(jnp.maximum, ss), axis=1, keepdims=True)
        m_new = jnp.maximum(m_old, jnp.broadcast_to(mx, shape))
        alpha = jnp.exp2(m_old - m_new)
        ps = [jnp.where(b == 0.0, jnp.exp2(s - m_new), 0.0) for s, b in zip(ss, bs)]
        psum = jnp.sum(functools.reduce(jnp.add, ps), axis=1, keepdims=True)
        l_ref[...] = l_ref[...] * alpha + jnp.broadcast_to(psum, shape)
        pv = functools.reduce(jnp.add, [_dot_nt(p.astype(MM), vv) for p, vv in zip(ps, vvs)])
        acc_ref[...] = acc_ref[...] * jnp.concatenate([alpha] * (W_A // LANES), axis=1) + pv
        m_ref[...] = m_new

    absorb([kp[0, 0].astype(MM) for kp in kps], [vp[0, 0].astype(MM) for vp in vps],
           [bias_ref[0, t] for t in range(pg)])

    @pl.when(j == n_s - 1)
    def _():
        absorb([kn_ref[0]], [vn_ref[0]], [bnew_ref[0, 0]])
        o = acc_ref[...] / jnp.concatenate([l_ref[...]] * (W_A // LANES), axis=1)
        o = jnp.where(own, o, 0.0)
        out = functools.reduce(jnp.add, [o[h * SUBLANES:(h + 1) * SUBLANES] for h in range(H_A)])
        o_ref[0] = out.astype(o_ref.dtype)


def _dsa_sample(layer, q16, qi16, sm32, sm16, k16, v16, cache_k, cache_v, cache_ki, page_table, bs, ts):
    n_pages = page_table.shape[1]
    page = cache_ki.shape[2]
    assert page == LANES and ts <= SUBLANES
    past = n_pages * page
    topk = min(TOPK, (past + ts) // 4)
    idx_bits = int(past + ts).bit_length() + 1
    pt = page_table.reshape(-1)
    rq = ts * H_I
    qi_r = qi16.reshape(bs, rq, D_I)
    wi = sm32[:, SM_WI:SM_WI + H_I] * ((D_I ** -0.5) * (H_I ** -0.5))
    w_r = jnp.broadcast_to(wi.reshape(bs, rq, 1), (bs, rq, LANES))
    pad_rows = lambda a, rows: jnp.pad(a.reshape(bs, ts, -1), ((0, 0), (0, rows - ts), (0, 0)))
    new_t = lambda a: jnp.pad(jnp.swapaxes(a.reshape(bs, ts, -1), 1, 2), ((0, 0), (0, 0), (0, page - ts)))
    kin = new_t(sm16[:, SM_KI:SM_KI + D_I])
    per_b = lambda shp: pl.BlockSpec((1,) + shp, lambda b, j, p: (b,) + (0,) * len(shp))
    pg = math.gcd(n_pages, 16)
    n_steps = n_pages // pg
    paged = lambda w, t: pl.BlockSpec(
        (1, 1, w, page), lambda b, j, p: (layer, p[b * n_pages + j * pg + t], 0, 0))
    tiles = n_pages + 1
    group = max(g for g in range(1, 9) if tiles % g == 0)

    bias = pl.pallas_call(
        functools.partial(_dsa_sample_select_kernel, topk=topk, n_q=ts, idx_bits=idx_bits, pg=pg, group=group),
        grid_spec=pltpu.PrefetchScalarGridSpec(
            num_scalar_prefetch=1, grid=(bs, n_steps),
            in_specs=[per_b((rq, D_I)), per_b((rq, LANES)), per_b((D_I, page))]
                     + [paged(D_I, t) for t in range(pg)],
            out_specs=per_b((tiles, SUBLANES, LANES)),
            scratch_shapes=[pltpu.VMEM((tiles, SUBLANES, LANES), I32)]),
        out_shape=jax.ShapeDtypeStruct((bs, tiles, SUBLANES, LANES), F32),
        compiler_params=_params(("parallel", "arbitrary")),
        name="dsa_sample_select",
    )(pt, qi_r, w_r, kin, *([jnp.swapaxes(cache_ki, 2, 3)] * pg))

    page_t = lambda c: jnp.transpose(c, (0, 1, 3, 4, 2)).reshape(c.shape[:2] + (W_A, page))
    ck = page_t(cache_k)
    cv = page_t(cache_v)
    rows = H_A * SUBLANES
    out = pl.pallas_call(
        functools.partial(_dsa_sample_attend_kernel, pg=pg),
        grid_spec=pltpu.PrefetchScalarGridSpec(
            num_scalar_prefetch=1, grid=(bs, n_steps),
            in_specs=[per_b((SUBLANES, W_A)),
                      pl.BlockSpec((1, pg, SUBLANES, LANES), lambda b, j, p: (b, j, 0, 0)),
                      pl.BlockSpec((1, 1, SUBLANES, LANES), lambda b, j, p: (b, n_pages, 0, 0)),
                      per_b((W_A, page)), per_b((W_A, page))]
                     + [paged(W_A, t) for t in range(pg)] * 2,
            out_specs=per_b((SUBLANES, W_A)),
            scratch_shapes=[pltpu.VMEM((rows, W_A), MM), pltpu.VMEM((rows, LANES), F32),
                            pltpu.VMEM((rows, LANES), F32), pltpu.VMEM((rows, W_A), F32)]),
        out_shape=jax.ShapeDtypeStruct((bs, SUBLANES, W_A), MM),
        compiler_params=_params(("parallel", "arbitrary")),
        name="dsa_sample_attend",
    )(pt, pad_rows(q16, SUBLANES), bias, bias, new_t(k16), new_t(v16), *([ck] * pg), *([cv] * pg))
    return out[:, :ts].reshape(bs * ts, W_A)


def _pad_seq(a, batch, t, t_pad):
    if t == t_pad:
        return a
    return jnp.pad(a.reshape(batch, t, -1), ((0, 0), (0, t_pad - t), (0, 0))).reshape(batch * t_pad, -1)


def _trunk(x, p, lw, conv_b0, ssm_b0, conv_c0, ssm_c0, kv, tm, tt):
    batch, t, d = x.shape
    n = batch * t
    depth = len(lw)
    t_pad = -(-t // tt) * tt
    tv_last = t - (t_pad - tt)
    assert min(CHUNK, tt) == DK_B == P_C and tv_last >= CONV_W - 1
    h = x.reshape(n, d)
    rec = [[] for _ in range(7)]
    for l, w in enumerate(lw):
        if kv is None:
            assert t % tm == 0
            (q16, k32, k16, v32, v16t, qi16, sm32, sm16, bqkv, bz, cz, cxbc, smt) = _inproj(
                h, w["g_mix"], w["w_in"], w["q_norm"], w["k_norm"], tm, feature_major_seq=t)
            ya = _dsa_prompt(q16, qi16, smt, k16, v16t, sm16, batch, t, tm)
            heads_last = lambda a: jnp.transpose(a.reshape(batch, H_A, DH_A, t), (0, 3, 1, 2))
            k_rows, v_rows = heads_last(k32), heads_last(v32)
        else:
            (q16, k32, k16, v32, v16, qi16, sm32, sm16, bqkv, bz, cz, cxbc) = _inproj(
                h, w["g_mix"], w["w_in"], w["q_norm"], w["k_norm"], tm)
            ya = _dsa_sample(l, q16, qi16, sm32, sm16, k16, v16, kv[0], kv[1], kv[2], kv[3], batch, t)
            k_rows, v_rows = k32.reshape(batch, t, H_A, DH_A), v32.reshape(batch, t, H_A, DH_A)
        pad = lambda a: _pad_seq(a, batch, t, t_pad)
        unpad = lambda a: a if t == t_pad else a.reshape(batch, t_pad, -1)[:, :t].reshape(n, -1)
        smp = pad(sm32)
        yb, bconv, bs_ = _gdn(pad(bqkv), pad(bz), smp, conv_b0[l], ssm_b0[l], w["b_conv_w"], w["b_A_log"],
                              w["b_dt_bias"], w["b_norm"], batch, t_pad, tt, tv_last)
        yc, cconv, cs_ = _ssd(pad(cxbc), pad(cz), smp, conv_c0[l], ssm_c0[l], w["c_conv_w"], w["c_conv_b"],
                              w["c_A_log"], w["c_dt_bias"], w["c_D"], w["c_norm"], batch, t_pad, tt, tv_last)
        h = _channel_mix(h, ya, unpad(yb), unpad(yc), p[l].reshape(n, -1), w["w_out"], w["g_ffn"], w["g_ple"],
                         w["w_ple_gate"], w["w_ple_proj"], tm, FF_TILE, ffn=w.get("ffn"), moe=w.get("moe"))
        for r, a in zip(rec, (k_rows, v_rows,
                              sm32[:, SM_KI:SM_KI + D_I].reshape(batch, t, D_I), bconv, bs_, cconv, cs_)):
            r.append(a)
    return h.reshape(batch, t, d), [jnp.stack(r) for r in rec]


def kernel(x_prompt, x_sample, cache_k, cache_v, cache_kidx, state_b_conv, state_b_ssm, state_c_conv, state_c_ssm, page_table, p_prompt, p_sample, w_in, w_out, g_mix, g_ffn, g_ple, q_norm, k_norm, b_conv_w, b_A_log, b_dt_bias, b_norm, c_conv_w, c_conv_b, c_A_log, c_dt_bias, c_D, c_norm, ffn_w1, ffn_w3, ffn_w2, moe_router, moe_w1, moe_w3, moe_w2, w_ple_gate, w_ple_proj):
    depth = w_in.shape[0]
    lw = []
    for l in range(depth):
        w = dict(w_in=_arrange_w_in(w_in[l]), w_out=w_out[l].astype(MM), g_mix=g_mix[l], g_ffn=g_ffn[l],
                 g_ple=g_ple[l], q_norm=q_norm[l], k_norm=k_norm[l], b_conv_w=b_conv_w[l], b_A_log=b_A_log[l],
                 b_dt_bias=b_dt_bias[l], b_norm=b_norm[l], c_conv_w=c_conv_w[l], c_conv_b=c_conv_b[l],
                 c_A_log=c_A_log[l], c_dt_bias=c_dt_bias[l], c_D=c_D[l], c_norm=c_norm[l],
                 w_ple_gate=w_ple_gate[l].astype(MM), w_ple_proj=w_ple_proj[l].astype(MM))
        j = l // 2
        if l % 2 == 0:
            w["ffn"] = (ffn_w1[j].astype(MM), ffn_w3[j].astype(MM), ffn_w2[j].astype(MM))
        else:
            w["moe"] = (moe_router[j], moe_w1[j].astype(MM), moe_w3[j].astype(MM), moe_w2[j].astype(MM))
        lw.append(w)

    bp, tp, _ = x_prompt.shape
    bs, ts, _ = x_sample.shape
    zeros = lambda *s: jnp.zeros((depth, bp) + s, F32)
    y_p, (k_p, v_p, ki_p, bc_p, bs_p, cc_p, cs_p) = _trunk(
        x_prompt, p_prompt, lw, zeros(CONV_W - 1, CB), zeros(H_B, DK_B, DV_B), zeros(CONV_W - 1, CC),
        zeros(H_C, P_C, N_C), None, min(512, bp * tp), min(256, tp))
    y_s, (k_s, v_s, ki_s, bc_s, bs_s, cc_s, cs_s) = _trunk(
        x_sample, p_sample, lw, state_b_conv, state_b_ssm, state_c_conv, state_c_ssm,
        (cache_k, cache_v, cache_kidx, page_table), bs * ts, CHUNK)
    return (y_p, y_s, k_p, k_s, v_p, v_s, ki_p, ki_s, bc_p, bc_s, bs_p, bs_s, cc_p, cc_s, cs_p, cs_s)
```

```python
import functools
import math

import numpy as np
import jax
import jax.numpy as jnp
from jax import lax
from jax.experimental import pallas as pl
from jax.experimental.pallas import tpu as pltpu

F32 = jnp.float32
I32 = jnp.int32
MM = jnp.bfloat16
EPS = 1e-6
NEG = -1e30
INT_MIN = -2 ** 31
LOG2E = 1.4426950408889634

LANES = 128
SUBLANES = 8
VMEM_LIMIT = 56 * 1024 * 1024

H_A, DH_A, H_I, D_I, TOPK, Q_BLOCK = 8, 64, 8, 64, 256, 128
H_B, DK_B, DV_B = 4, 64, 64
H_C, P_C, G_C, N_C = 4, 64, 2, 128
CONV_W, CHUNK = 4, 64
N_EXPERTS, TOP_E = 8, 2
FF_TILE = 1792
W_A, W_B, W_C = H_A * DH_A, H_B * DV_B, H_C * P_C
CB = 2 * H_B * DK_B + H_B * DV_B
CC = W_C + 2 * G_C * N_C

SM_KI, SM_WI, SM_BETA, SM_A, SM_DT = 0, 64, 72, 76, 80


def _dot(a, b):
    return jnp.dot(a, b, preferred_element_type=F32)


def _dot_nt(a, b):
    return lax.dot_general(a, b, (((1,), (1,)), ((), ())), preferred_element_type=F32)


def _dot_tn(a, b):
    return lax.dot_general(a, b, (((0,), (0,)), ((), ())), preferred_element_type=F32)


def _split2(x):
    hi = x.astype(MM)
    lo = (x - hi.astype(F32)).astype(MM)
    return hi, lo


def _split3(x):
    a1 = x.astype(MM)
    r = x - a1.astype(F32)
    a2 = r.astype(MM)
    a3 = (r - a2.astype(F32)).astype(MM)
    return a1, a2, a3


def _dot_hi(a, b, f=_dot):
    ah, al = _split2(a)
    bh, bl = _split2(b)
    return f(ah, bh) + (f(ah, bl) + f(al, bh))


def _dot_x3(a, b_exact):
    a1, a2, a3 = _split3(a)
    return _dot(a1, b_exact) + (_dot(a2, b_exact) + _dot(a3, b_exact))


def _x3_dot(a_exact, b):
    b1, b2, b3 = _split3(b)
    return _dot(a_exact, b1) + (_dot(a_exact, b2) + _dot(a_exact, b3))


def _group_mean(xx, bd):
    hi, lo = _split2(xx)
    return _dot(hi, bd) + _dot(lo, bd)


def _sigmoid(x):
    return 1.0 / (1.0 + jnp.exp(-x))


def _silu(x):
    return x * _sigmoid(x)


def _softplus(x):
    return jnp.maximum(x, 0.0) + jnp.log(1.0 + jnp.exp(-jnp.abs(x)))


def _rms(x, g):
    ms = jnp.mean(x * x, axis=-1, keepdims=True)
    return x * lax.rsqrt(ms + EPS) * g


def _block_diag(width, group, value):
    i = np.arange(width)
    return jnp.asarray(np.where((i[:, None] // group) == (i[None, :] // group), value, 0.0), MM)


def _params(sem):
    return pltpu.CompilerParams(dimension_semantics=sem, vmem_limit_bytes=VMEM_LIMIT)


def _full(shape):
    n = len(shape)
    return pl.BlockSpec(shape, lambda *_: (0,) * n, pipeline_mode=pl.Buffered(1))


def _inproj_kernel(h_ref, g_ref, w_ref, qn_ref, kn_ref, bd_ref,
                   q16_ref, k32_ref, k16_ref, v32_ref, v16_ref, qi16_ref,
                   sm32_ref, sm16_ref, bqkv_ref, bz_ref, cz_ref, cxbc_ref, *smt_ref, feature_major):
    u = _rms(h_ref[...], g_ref[...]).astype(MM)

    def proj(a, b):
        return _dot_nt(u, w_ref[a:b, :])

    bd = bd_ref[...]
    aq = proj(0, W_A)
    q = aq * lax.rsqrt(_group_mean(aq * aq, bd) + EPS) * qn_ref[...]
    q16_ref[...] = (q * (DH_A ** -0.5 * LOG2E)).astype(MM)
    ak = proj(W_A, 2 * W_A)
    k = ak * lax.rsqrt(_group_mean(ak * ak, bd) + EPS) * kn_ref[...]
    k16_ref[...] = k.astype(MM)
    v = proj(2 * W_A, 3 * W_A)
    qi16_ref[...] = proj(3 * W_A, 4 * W_A).astype(MM)
    o = 4 * W_A
    sm = proj(o, o + LANES)
    sm32_ref[...] = sm
    sm16_ref[...] = sm.astype(MM)
    if feature_major:
        vt = v.T
        k32_ref[0] = k.T
        v32_ref[0] = vt
        v16_ref[0] = vt.astype(MM)
        smt_ref[0][...] = sm.T
    else:
        k32_ref[...] = k
        v32_ref[...] = v
        v16_ref[...] = v.astype(MM)
    o += LANES
    bqkv_ref[...] = proj(o, o + CB)
    bz_ref[...] = proj(o + CB, o + CB + W_B)
    o += CB + W_B
    cz_ref[...] = proj(o, o + W_C)
    cxbc_ref[...] = proj(o + W_C, o + W_C + CC)


def _arrange_w_in(w):
    w = w.T
    o_aik = 4 * W_A
    o_aiw = o_aik + D_I
    o_bqkv = o_aiw + H_I
    o_bz = o_bqkv + CB
    o_bbeta = o_bz + W_B
    o_ba = o_bbeta + H_B
    o_cz = o_ba + H_B
    o_cxbc = o_cz + W_C
    o_cdt = o_cxbc + CC
    small = jnp.concatenate(
        [w[o_aik:o_aiw], w[o_aiw:o_bqkv], w[o_bbeta:o_ba], w[o_ba:o_cz], w[o_cdt:o_cdt + H_C]], axis=0)
    small = jnp.pad(small, ((0, LANES - small.shape[0]), (0, 0)))
    return jnp.concatenate([w[:o_aik], small, w[o_bqkv:o_bbeta], w[o_cz:o_cdt]], axis=0).astype(MM)


def _inproj(h, g, w_arr, qn, kn, tm, feature_major_seq=None):
    n, d = h.shape
    wtot = w_arr.shape[0]
    bd = _block_diag(W_A, DH_A, 1.0 / DH_A)
    row = lambda w: pl.BlockSpec((tm, w), lambda i: (i, 0))
    rowout = lambda w, t: (row(w), jax.ShapeDtypeStruct((n, w), t))
    outs = [rowout(W_A, MM), rowout(W_A, F32), rowout(W_A, MM), rowout(W_A, F32), rowout(W_A, MM),
            rowout(W_A, MM), rowout(LANES, F32), rowout(LANES, MM), rowout(CB, F32), rowout(W_B, F32),
            rowout(W_C, F32), rowout(CC, F32)]
    if feature_major_seq is not None:
        tpb = feature_major_seq // tm
        per_seq = (pl.BlockSpec((1, W_A, tm), lambda i: (i // tpb, 0, i % tpb)),
                   jax.ShapeDtypeStruct((n // feature_major_seq, W_A, feature_major_seq), F32))
        outs[1] = per_seq
        outs[3] = per_seq
        outs[4] = (pl.BlockSpec((1, W_A, tm), lambda i: (i, 0, 0)),
                   jax.ShapeDtypeStruct((n // tm, W_A, tm), MM))
        outs.append((pl.BlockSpec((LANES, tm), lambda i: (0, i)), jax.ShapeDtypeStruct((LANES, n), F32)))
    return pl.pallas_call(
        functools.partial(_inproj_kernel, feature_major=feature_major_seq is not None),
        grid=(n // tm,),
        in_specs=[row(d), _full((1, d)), _full((wtot, d)), _full((1, W_A)), _full((1, W_A)),
                  _full((W_A, W_A))],
        out_specs=[s for s, _ in outs],
        out_shape=[t for _, t in outs],
        compiler_params=_params(("parallel",)),
        name="inproj",
    )(h, g.reshape(1, d), w_arr, jnp.tile(qn, H_A).reshape(1, W_A), jnp.tile(kn, H_A).reshape(1, W_A), bd)


def _mix_prologue(h_ref, ya_ref, yb_ref, yc_ref, wo_ref, gf_ref, h1_ref, u2_ref, acc_ref):
    h1 = h_ref[...] + (_dot(ya_ref[...], wo_ref[0:W_A, :])
                       + _dot(yb_ref[...], wo_ref[W_A:W_A + W_B, :])
                       + _dot(yc_ref[...], wo_ref[W_A + W_B:, :]))
    h1_ref[...] = h1
    u2 = _rms(h1, gf_ref[...])
    u2_ref[...] = u2.astype(MM)
    acc_ref[...] = jnp.zeros_like(acc_ref)
    return u2


def _mix_epilogue(p_ref, gp_ref, wg_ref, wp_ref, h1_ref, acc_ref, o_ref):
    h2 = h1_ref[...] + acc_ref[...]
    u3 = _rms(h2, gp_ref[...]).astype(MM)
    gate = _sigmoid(_dot(u3, wg_ref[...]))
    o_ref[...] = h2 + gate * _dot(p_ref[...].astype(MM), wp_ref[...])


def _ffn_kernel(h_ref, ya_ref, yb_ref, yc_ref, p_ref, wo_ref, gf_ref, w1_ref, w3_ref, w2_ref,
                gp_ref, wg_ref, wp_ref, o_ref, h1_ref, u2_ref, acc_ref):
    f = pl.program_id(1)

    @pl.when(f == 0)
    def _():
        _mix_prologue(h_ref, ya_ref, yb_ref, yc_ref, wo_ref, gf_ref, h1_ref, u2_ref, acc_ref)

    u2 = u2_ref[...]
    hid = _silu(_dot(u2, w1_ref[...])) * _dot(u2, w3_ref[...])
    acc_ref[...] += _dot(hid.astype(MM), w2_ref[...])

    @pl.when(f == pl.num_programs(1) - 1)
    def _():
        _mix_epilogue(p_ref, gp_ref, wg_ref, wp_ref, h1_ref, acc_ref, o_ref)


def _moe_kernel(h_ref, ya_ref, yb_ref, yc_ref, p_ref, wo_ref, gf_ref, wr_ref, w1_ref, w3_ref, w2_ref,
                gp_ref, wg_ref, wp_ref, o_ref, h1_ref, u2_ref, acc_ref, comb_ref):
    e = pl.program_id(1)
    f = pl.program_id(2)

    @pl.when((e == 0) & (f == 0))
    def _():
        u2 = _mix_prologue(h_ref, ya_ref, yb_ref, yc_ref, wo_ref, gf_ref, h1_ref, u2_ref, acc_ref)
        logits = _dot_hi(u2, wr_ref[...])
        lane = lax.broadcasted_iota(I32, logits.shape, 1)
        big = jnp.int32(LANES)
        lg = jnp.where(lane < N_EXPERTS, logits, -jnp.inf)
        m1 = jnp.max(lg, axis=1, keepdims=True)
        i1 = jnp.min(jnp.where(lg == m1, lane, big), axis=1, keepdims=True)
        lg2 = jnp.where(lane == i1, -jnp.inf, lg)
        m2 = jnp.max(lg2, axis=1, keepdims=True)
        i2 = jnp.min(jnp.where(lg2 == m2, lane, big), axis=1, keepdims=True)
        e2 = jnp.exp(m2 - m1)
        g1 = 1.0 / (1.0 + e2)
        g2 = e2 / (1.0 + e2)
        for ex in range(N_EXPERTS):
            ce = jnp.where(i1 == ex, g1, 0.0) + jnp.where(i2 == ex, g2, 0.0)
            comb_ref[ex] = jnp.broadcast_to(ce, comb_ref.shape[1:])

    u2 = u2_ref[...]
    hid = _silu(_dot(u2, w1_ref[0])) * _dot(u2, w3_ref[0])
    tf = hid.shape[1]
    ce = comb_ref[e]
    hid = hid * jnp.concatenate([ce] * (tf // LANES), axis=1)
    acc_ref[...] += _dot(hid.astype(MM), w2_ref[0])

    @pl.when((e == pl.num_programs(1) - 1) & (f == pl.num_programs(2) - 1))
    def _():
        _mix_epilogue(p_ref, gp_ref, wg_ref, wp_ref, h1_ref, acc_ref, o_ref)


def _channel_mix(h, ya, yb, yc, p, wo, gf, gp, wg, wp, tm, tf, *, ffn=None, moe=None):
    n, d = h.shape
    dple = p.shape[1]
    if ffn is not None:
        w1, w3, w2 = ffn
        dff = w1.shape[1]
        grid = (n // tm, dff // tf)
        row = lambda w: pl.BlockSpec((tm, w), lambda i, f: (i, 0))
        wspecs = [pl.BlockSpec((d, tf), lambda i, f: (0, f)), pl.BlockSpec((d, tf), lambda i, f: (0, f)),
                  pl.BlockSpec((tf, d), lambda i, f: (f, 0))]
        wargs = [w1, w3, w2]
        body = _ffn_kernel
        extra_scratch = []
        sem = ("parallel", "arbitrary")
    else:
        wr, w1, w3, w2 = moe
        ne, _, dff = w1.shape
        grid = (n // tm, ne, dff // tf)
        row = lambda w: pl.BlockSpec((tm, w), lambda i, e, f: (i, 0))
        wspecs = [_full((d, LANES)),
                  pl.BlockSpec((1, d, tf), lambda i, e, f: (e, 0, f)),
                  pl.BlockSpec((1, d, tf), lambda i, e, f: (e, 0, f)),
                  pl.BlockSpec((1, tf, d), lambda i, e, f: (e, f, 0))]
        wargs = [jnp.pad(wr, ((0, 0), (0, LANES - ne))), w1, w3, w2]
        body = _moe_kernel
        extra_scratch = [pltpu.VMEM((ne, tm, LANES), F32)]
        sem = ("parallel", "arbitrary", "arbitrary")
    return pl.pallas_call(
        body,
        grid=grid,
        in_specs=[row(d), row(W_A), row(W_B), row(W_C), row(dple), _full((d, d)), _full((1, d))]
                 + wspecs + [_full((1, d)), _full((d, d)), _full((dple, d))],
        out_specs=row(d),
        out_shape=jax.ShapeDtypeStruct((n, d), F32),
        scratch_shapes=[pltpu.VMEM((tm, d), F32), pltpu.VMEM((tm, d), MM), pltpu.VMEM((tm, d), F32)]
                       + extra_scratch,
        compiler_params=_params(sem),
        name="ffn" if ffn is not None else "moe",
    )(h, ya, yb, yc, p, wo, gf.reshape(1, d), *wargs, gp.reshape(1, d), wg, wp)


def _sortable(x):
    b = lax.bitcast_convert_type(x, I32)
    return b ^ ((b >> 31) & jnp.int32(0x7FFFFFFF))


def _select_topk(keys_ref, bias_ref, n_groups, group, topk, idx_bits, tile_shape, key_axis):
    imin = jnp.int32(INT_MIN)
    pos = lax.broadcasted_iota(I32, tile_shape, key_axis)
    if key_axis == 1:
        qshape = tile_shape
        full = lambda x: x
        acc0 = jnp.zeros(tile_shape, F32)
        add_tile = lambda acc, m: acc + m
        finish = lambda acc: jnp.broadcast_to(jnp.sum(acc, axis=1, keepdims=True), tile_shape)
    else:
        qshape = (1, tile_shape[1])
        full = lambda x: jnp.broadcast_to(x, tile_shape)
        acc0 = jnp.zeros((SUBLANES, tile_shape[1]), F32)

        def add_tile(acc, m):
            parts = [m[s * SUBLANES:(s + 1) * SUBLANES] for s in range(tile_shape[0] // SUBLANES)]
            while len(parts) > 1:
                parts = [a + b for a, b in zip(parts[::2], parts[1::2])]
            return acc + parts[0]

        finish = lambda acc: jnp.sum(acc, axis=0, keepdims=True)

    def count(pred):
        def body(i, acc):
            for t in range(group):
                j = i * group + t
                acc = add_tile(acc, jnp.where(pred(keys_ref[j], j), 1.0, 0.0))
            return acc
        return finish(lax.fori_loop(0, n_groups, body, acc0))

    def value_bit(b, carry):
        thr, cnt_thr = carry
        cand = thr + jnp.left_shift(jnp.int32(1), 31 - b)
        cand_f = full(cand)
        cnt = count(lambda k, j: k >= cand_f)
        keep = cnt >= topk
        return jnp.where(keep, cand, thr), jnp.where(keep, cnt, cnt_thr)

    thr, cnt_thr = lax.fori_loop(0, 32, value_bit, (jnp.full(qshape, imin), jnp.full(qshape, 2.0 ** 30, F32)))
    no_cut = jnp.full(qshape, 2 ** idx_bits - 1, I32)
    tied = (cnt_thr > topk) & (thr > imin)
    floor = jnp.maximum(thr, imin + 1)
    floor_f = full(floor)

    def index_cut():
        need = topk - count(lambda k, j: k > floor_f)

        def index_bit(b, c):
            cc = c | jnp.left_shift(jnp.int32(1), idx_bits - 1 - b)
            cc_f = full(cc)
            cnt = count(lambda k, j: (k == floor_f) & (j * LANES + pos < cc_f))
            return jnp.where(cnt <= need, cc, c)

        return lax.fori_loop(0, idx_bits, index_bit, jnp.zeros(qshape, I32))

    cut = lax.cond(jnp.max(jnp.where(tied, 1.0, 0.0)) > 0.0, index_cut, lambda: no_cut)
    cut_f = full(jnp.where(tied, cut, no_cut))

    def write(i, carry):
        for t in range(group):
            j = i * group + t
            k = keys_ref[j]
            sel = (k > floor_f) | ((k == floor_f) & (j * LANES + pos < cut_f))
            bias_ref[j] = jnp.where(sel, 0.0, NEG)
        return carry

    lax.fori_loop(0, n_groups, write, 0)


def _two_buffer_pipeline(n, products, finish, back, bufs):
    a, b = bufs
    finish(0, products(0), a)

    def step(c, cur, nxt):
        @pl.when(c + 1 < n)
        def _():
            back(c, cur, lambda: products(c + 1), lambda vals: finish(c + 1, vals, nxt))

        @pl.when(c + 1 == n)
        def _():
            back(c, cur, lambda: None, lambda vals: None)

    def body(k, carry):
        step(2 * k, a, b)
        step(2 * k + 1, b, a)
        return carry

    lax.fori_loop(0, (n + 1) // 2, body, 0)


def _dsa_prompt_kernel(q_ref, qi_ref, smt_ref, k_ref, vt_ref, ki_ref, o_ref,
                       keys_ref, bias_ref, qp_ref, qip_ref, m_ref, mb_ref, l_ref, acc_ref,
                       sta_ref, stb_ref, *, topk, kc, idx_bits):
    i = pl.program_id(1)
    qb = q_ref.shape[0]
    tpc = kc // LANES
    n_chunks = ((i + 1) * qb + kc - 1) // kc
    n_pairs = H_A // 2
    tile = (LANES, qb)
    key_pos = lax.broadcasted_iota(I32, tile, 0)
    q_pos = i * qb + lax.broadcasted_iota(I32, tile, 1)
    low = lax.broadcasted_iota(I32, (qb, LANES), 1) < DH_A

    for h in range(H_A):
        pr, half = h // 2, h % 2
        rows = slice(half * qb, (half + 1) * qb)
        blk = q_ref[:, pr * LANES:(pr + 1) * LANES]
        qp_ref[pr, rows, :] = jnp.where(low if half == 0 else ~low, blk, jnp.zeros_like(blk))
        blk = qi_ref[:, pr * LANES:(pr + 1) * LANES]
        if half == 1:
            blk = pltpu.roll(blk, DH_A, 1)
        qip_ref[pr, rows, :] = jnp.where(low, blk, jnp.zeros_like(blk))
    wt = smt_ref[SM_WI:SM_WI + H_I, :] * ((D_I ** -0.5) * (H_I ** -0.5))

    def score_chunk(c, carry):
        base = pl.multiple_of(c * kc, kc)
        kt = ki_ref[pl.ds(base, kc), :]
        ss = [_dot_nt(kt, qip_ref[pr]) for pr in range(n_pairs)]
        for t in range(tpc):
            sc = jnp.zeros(tile, F32)
            for h in range(H_I):
                blk = ss[h // 2][t * LANES:(t + 1) * LANES, (h % 2) * qb:(h % 2 + 1) * qb]
                sc = sc + wt[h:h + 1, :] * jnp.maximum(blk, 0.0)
            valid = base + t * LANES + key_pos <= q_pos
            keys_ref[c * tpc + t] = jnp.where(valid, _sortable(sc), jnp.int32(INT_MIN))
        return carry

    lax.fori_loop(0, n_chunks, score_chunk, 0)
    _select_topk(keys_ref, bias_ref, n_chunks, tpc, topk, idx_bits, tile, 0)

    m_ref[...] = jnp.full(m_ref.shape, NEG, F32)
    mb_ref[...] = jnp.full(mb_ref.shape, NEG, F32)
    l_ref[...] = jnp.zeros(l_ref.shape, F32)
    acc_ref[...] = jnp.zeros(acc_ref.shape, F32)

    def products(c):
        base = pl.multiple_of(c * kc, kc)
        return [_dot_nt(k_ref[pl.ds(base, kc), pr * LANES:(pr + 1) * LANES], qp_ref[pr])
                for pr in range(n_pairs)]

    def front(c, prods, buf):
        for pr in range(n_pairs):
            st = []
            for t in range(tpc):
                b = bias_ref[c * tpc + t]
                x = prods[pr][t * LANES:(t + 1) * LANES, :] + jnp.concatenate([b, b], axis=1)
                buf[pr, t * LANES:(t + 1) * LANES, :] = x
                st.append(x)
            cmax = jnp.max(functools.reduce(jnp.maximum, st), axis=0, keepdims=True)
            m_ref[pr] = jnp.maximum(m_ref[pr], cmax)

    def back(c, buf, start_next, finish_next):
        m_now = [m_ref[pr] for pr in range(n_pairs)]
        nxt = start_next()
        for pr in range(n_pairs):
            alpha = jnp.exp2(mb_ref[pr] - m_now[pr])
            ps = [jnp.exp2(buf[pr, t * LANES:(t + 1) * LANES, :] - m_now[pr]) for t in range(tpc)]
            l_ref[pr] = l_ref[pr] * alpha + jnp.sum(functools.reduce(jnp.add, ps), axis=0, keepdims=True)
            p = jnp.concatenate(ps, axis=0).astype(MM)
            acc_ref[pr] = acc_ref[pr] * alpha + _dot(vt_ref[c, pr * LANES:(pr + 1) * LANES, :], p)
            mb_ref[pr] = m_now[pr]
        finish_next(nxt)

    _two_buffer_pipeline(n_chunks, products, front, back, (sta_ref, stb_ref))
    for pr in range(n_pairs):
        o = acc_ref[pr] / l_ref[pr]
        ot = jnp.concatenate([o[:DH_A, :qb], o[DH_A:, qb:]], axis=0)
        o_ref[:, pr * LANES:(pr + 1) * LANES] = ot.T.astype(o_ref.dtype)


def _dsa_prompt(q16, qi16, smt, k16, vt, sm16, batch, seq, kc):
    qb = Q_BLOCK
    topk = min(TOPK, seq // 4)
    nq = seq // qb
    idx_bits = int(seq).bit_length() + 1
    assert H_A == H_I and DH_A == D_I and qb == LANES
    n_kc = seq // kc
    n_pairs = H_A // 2
    blk = lambda w: pl.BlockSpec((qb, w), lambda b, i: (b * nq + i, 0))
    seqblk = lambda w: pl.BlockSpec((seq, w), lambda b, i: (b, 0), pipeline_mode=pl.Buffered(1))
    tiles = (seq // LANES, LANES, qb)
    return pl.pallas_call(
        functools.partial(_dsa_prompt_kernel, topk=topk, kc=kc, idx_bits=idx_bits),
        grid=(batch, nq),
        in_specs=[blk(W_A), blk(W_A), pl.BlockSpec((LANES, qb), lambda b, i: (0, b * nq + i)), seqblk(W_A),
                  pl.BlockSpec((n_kc, W_A, kc), lambda b, i: (b, 0, 0), pipeline_mode=pl.Buffered(1)),
                  seqblk(LANES)],
        out_specs=blk(W_A),
        out_shape=jax.ShapeDtypeStruct((batch * seq, W_A), MM),
        scratch_shapes=[pltpu.VMEM(tiles, I32), pltpu.VMEM(tiles, F32),
                        pltpu.VMEM((n_pairs, 2 * qb, LANES), MM), pltpu.VMEM((n_pairs, 2 * qb, LANES), MM),
                        pltpu.VMEM((n_pairs, 1, 2 * qb), F32), pltpu.VMEM((n_pairs, 1, 2 * qb), F32),
                        pltpu.VMEM((n_pairs, 1, 2 * qb), F32), pltpu.VMEM((n_pairs, LANES, 2 * qb), F32),
                        pltpu.VMEM((n_pairs, kc, 2 * qb), F32), pltpu.VMEM((n_pairs, kc, 2 * qb), F32)],
        compiler_params=_params(("parallel", "arbitrary")),
        name="dsa_prompt",
    )(q16, qi16, smt, k16, vt, sm16)


def _expand_matrix(first_lane, heads, width):
    m = np.zeros((LANES, heads * width), np.float32)
    for h in range(heads):
        m[first_lane + h, h * width:(h + 1) * width] = 1.0
    return jnp.asarray(m, MM)


def _chunk_masks(tt, chunk):
    i = np.arange(tt)
    same = (i[:, None] // chunk) == (i[None, :] // chunk)
    ltri = jnp.asarray(np.where(same & (i[:, None] >= i[None, :]), 1.0, 0.0), MM)
    utri = jnp.asarray(np.where(same & (i[:, None] <= i[None, :]), 1.0, 0.0), F32)
    return ltri, utri


def _lane_vec(values, first_lane):
    return jnp.zeros((1, LANES), F32).at[0, first_lane:first_lane + values.shape[0]].set(values)


def _conv_tile(x_ref, buf_ref, cw_ref, xe_ref, nb_ref, t, n_t, tv_last):
    tt = x_ref.shape[0]
    hist = SUBLANES

    @pl.when(t == 0)
    def _():
        xe_ref[0:hist, :] = jnp.zeros((hist, xe_ref.shape[1]), F32)
        xe_ref[hist - (CONV_W - 1):hist, :] = buf_ref[0]

    xe_ref[hist:hist + tt, :] = x_ref[...]
    y = jnp.zeros(x_ref.shape, F32)
    for i in range(CONV_W):
        o = hist - (CONV_W - 1) + i
        y = y + xe_ref[o:o + tt, :] * cw_ref[i:i + 1, :]

    @pl.when(t == n_t - 1)
    def _():
        nb_ref[0] = xe_ref[hist + tv_last - (CONV_W - 1):hist + tv_last, :]

    xe_ref[0:hist, :] = xe_ref[tt:tt + hist, :]
    return y


def _live_rows(shape, t, n_t, tv_last):
    row = lax.broadcasted_iota(I32, shape, 0)
    return (t < n_t - 1) | (row < tv_last)


def _row_cumsum(vals, utri, ones_c, heads, width, chunk):
    tt = vals.shape[0]
    out = []
    for h in range(heads):
        col = vals[:, h * width:h * width + chunk]
        out.append(_x3_dot(ones_c, jnp.concatenate([col] * (tt // chunk), axis=1) * utri))
    return out


def _gdn_kernel(x_ref, z_ref, sm_ref, buf_ref, s0_ref, cw_ref, alog_ref, dtb_ref, ng_ref,
                ebeta_ref, eg_ref, ltri_ref, utri_ref, bd1_ref, bdn_ref,
                y_ref, nb_ref, st_ref, xe_ref, s_ref, o_ref, *, tv_last, chunk):
    t = pl.program_id(1)
    n_t = pl.num_programs(1)
    tt = x_ref.shape[0]
    wq = H_B * DK_B

    @pl.when(t == 0)
    def _():
        s_ref[...] = s0_ref[0]

    qkv = _silu(_conv_tile(x_ref, buf_ref, cw_ref, xe_ref, nb_ref, t, n_t, tv_last))
    qk = qkv[:, :2 * wq]
    qk = qk * lax.rsqrt(_group_mean(qk * qk, bd1_ref[...]) + EPS)
    q = qk[:, :wq] * (DK_B ** -0.5)
    k = qk[:, wq:]
    v = qkv[:, 2 * wq:]

    sm = sm_ref[...]
    beta = _sigmoid(sm)
    g = -jnp.exp(alog_ref[...]) * _softplus(sm + dtb_ref[...])
    if tv_last < tt:
        live = _live_rows(sm.shape, t, n_t, tv_last)
        beta = jnp.where(live, beta, 0.0)
        g = jnp.where(live, g, 0.0)
    betab = _dot_x3(beta, ebeta_ref[...])
    gb = _dot_x3(g, eg_ref[...])
    gc = _x3_dot(ltri_ref[...], gb)
    ones_c = jnp.ones((chunk, tt), MM)
    gr = _row_cumsum(gb, utri_ref[...], ones_c, H_B, DK_B, chunk)
    eg = jnp.exp(gc)
    kb = k * betab
    vb = v * betab
    kbeg = kb * eg
    qg = q * eg
    ii = lax.broadcasted_iota(I32, (chunk, chunk), 0)
    jj = lax.broadcasted_iota(I32, (chunk, chunk), 1)
    levels = max(1, (chunk - 1).bit_length())

    n_c = tt // chunk
    units = [(c, h) for c in range(n_c) for h in range(H_B)]
    rows = lambda c: slice(c * chunk, (c + 1) * chunk)
    cols = lambda h: slice(h * DK_B, (h + 1) * DK_B)
    k16 = k.astype(MM)
    kk = {u: _dot_nt(kb[rows(u[0]), cols(u[1])].astype(MM), k16[rows(u[0]), cols(u[1])]) for u in units}
    qk = {u: _dot_nt(q[rows(u[0]), cols(u[1])].astype(MM), k16[rows(u[0]), cols(u[1])]) for u in units}
    xs, ps, attn = {}, {}, {}
    for u in units:
        c, h = u
        decay = jnp.exp(jnp.where(ii >= jj, gc[rows(c), cols(h)] - gr[h][:, rows(c)], NEG))
        ps[u] = -jnp.where(ii > jj, kk[u] * decay, 0.0)
        attn[u] = (qk[u] * decay).astype(MM)
        xs[u] = jnp.concatenate([vb[rows(c), cols(h)], kbeg[rows(c), cols(h)]], axis=1)
    wx = 2 * DV_B
    for lv in range(levels):
        last_lv = lv == levels - 1
        prod = {}
        for u in units:
            ph, pl_ = _split2(ps[u])
            xh, xl = _split2(xs[u])
            bh = xh if last_lv else jnp.concatenate([xh, ph], axis=1)
            bl = xl if last_lv else jnp.concatenate([xl, pl_], axis=1)
            prod[u] = _dot(ph, bh) + (_dot(ph, bl) + _dot(pl_, bh))
        for u in units:
            xs[u] = xs[u] + prod[u][:, :wx]
            if not last_lv:
                ps[u] = prod[u][:, wx:]

    for c in range(n_c):
        last = gc[(c + 1) * chunk - 1:(c + 1) * chunk, :]
        kg = (k[rows(c), :] * jnp.exp(last - gc[rows(c), :])).astype(MM)
        gl = jnp.exp(last)
        qg16 = qg[rows(c), :].astype(MM)
        s_old = [s_ref[h] for h in range(H_B)]
        s16 = [s.astype(MM) for s in s_old]
        ws = [_dot(xs[c, h][:, DV_B:].astype(MM), s16[h]) for h in range(H_B)]
        qs = [_dot(qg16[:, cols(h)], s16[h]) for h in range(H_B)]
        u16 = [(xs[c, h][:, :DV_B] - ws[h]).astype(MM) for h in range(H_B)]
        au = [_dot(attn[c, h], u16[h]) for h in range(H_B)]
        ku = [_dot_tn(kg[:, cols(h)], u16[h]) for h in range(H_B)]
        for h in range(H_B):
            o_ref[rows(c), cols(h)] = qs[h] + au[h]
            s_ref[h] = s_old[h] * gl[:, h * DK_B:h * DK_B + 1] + ku[h]

    o = o_ref[...]
    o = o * lax.rsqrt(_group_mean(o * o, bdn_ref[...]) + EPS) * ng_ref[...]
    y_ref[...] = (o * _silu(z_ref[...])).astype(y_ref.dtype)

    @pl.when(t == n_t - 1)
    def _():
        st_ref[0] = s_ref[...]


def _ssd_kernel(x_ref, z_ref, sm_ref, buf_ref, h0_ref, cw_ref, cb_ref, alog_ref, dtb_ref, d_ref, ng_ref,
                edt_ref, ltri_ref, utri_ref, bdn_ref,
                y_ref, nb_ref, ht_ref, xe_ref, h_ref, o_ref, *, tv_last, chunk):
    t = pl.program_id(1)
    n_t = pl.num_programs(1)
    tt = x_ref.shape[0]
    gw = G_C * N_C
    hpg = H_C // G_C

    @pl.when(t == 0)
    def _():
        h_ref[...] = h0_ref[0]

    xbc = _silu(_conv_tile(x_ref, buf_ref, cw_ref, xe_ref, nb_ref, t, n_t, tv_last) + cb_ref[...])
    xs = xbc[:, :W_C]
    bm = xbc[:, W_C:W_C + gw].astype(MM)
    cm = xbc[:, W_C + gw:].astype(MM)

    dt = _softplus(sm_ref[...] + dtb_ref[...])
    if tv_last < tt:
        dt = jnp.where(_live_rows(dt.shape, t, n_t, tv_last), dt, 0.0)
    dtb = _dot_x3(dt, edt_ref[...])
    a = dtb * (-jnp.exp(alog_ref[...]))
    xdt = xs * dtb
    ac = _x3_dot(ltri_ref[...], a)
    ones_c = jnp.ones((chunk, tt), MM)
    ar = _row_cumsum(a, utri_ref[...], ones_c, H_C, P_C, chunk)
    ea = jnp.exp(ac)
    ii = lax.broadcasted_iota(I32, (chunk, chunk), 0)
    jj = lax.broadcasted_iota(I32, (chunk, chunk), 1)

    for c in range(tt // chunk):
        rs = slice(c * chunk, (c + 1) * chunk)
        last = ac[(c + 1) * chunk - 1:(c + 1) * chunk, :]
        xdec = (xdt[rs, :] * jnp.exp(last - ac[rs, :])).astype(MM)
        hdec = jnp.exp(last)
        for g in range(G_C):
            gs = slice(g * N_C, (g + 1) * N_C)
            cbm = _dot_nt(cm[rs, gs], bm[rs, gs])
            for h in range(g * hpg, (g + 1) * hpg):
                hs = slice(h * P_C, (h + 1) * P_C)
                lmat = jnp.exp(jnp.where(ii >= jj, ac[rs, hs] - ar[h][:, rs], NEG))
                hprev = h_ref[h]
                y = _dot((cbm * lmat).astype(MM), xdt[rs, hs].astype(MM))
                y = y + ea[rs, hs] * _dot_nt(cm[rs, gs], hprev.astype(MM))
                o_ref[rs, hs] = y + d_ref[:, hs] * xs[rs, hs]
                h_ref[h] = hprev * hdec[:, h * P_C:h * P_C + 1] + _dot_tn(xdec[:, hs], bm[rs, gs])

    yz = o_ref[...] * _silu(z_ref[...])
    y_ref[...] = (yz * lax.rsqrt(_group_mean(yz * yz, bdn_ref[...]) + EPS) * ng_ref[...]).astype(y_ref.dtype)

    @pl.when(t == n_t - 1)
    def _():
        ht_ref[0] = h_ref[...]


def _recurrent_call(body, name, x, z, sm, buf, state0, vecs, consts, batch, t_pad, tt, tv_last, width_out):
    n_t = t_pad // tt
    cw = x.shape[1]
    blk = lambda w: pl.BlockSpec((tt, w), lambda b, t: (b * n_t + t, 0))
    per_b = lambda shp: pl.BlockSpec((1,) + shp, lambda b, t: (b,) + (0,) * len(shp))
    sshape = state0.shape[1:]
    return pl.pallas_call(
        functools.partial(body, tv_last=tv_last, chunk=min(CHUNK, tt)),
        grid=(batch, n_t),
        in_specs=[blk(cw), blk(width_out), blk(LANES), per_b((CONV_W - 1, cw)), per_b(sshape)]
                 + [_full(v.shape) for v in vecs] + [_full(c.shape) for c in consts],
        out_specs=[blk(width_out), per_b((CONV_W - 1, cw)), per_b(sshape)],
        out_shape=[jax.ShapeDtypeStruct((batch * t_pad, width_out), MM),
                   jax.ShapeDtypeStruct((batch, CONV_W - 1, cw), F32),
                   jax.ShapeDtypeStruct((batch,) + sshape, F32)],
        scratch_shapes=[pltpu.VMEM((tt + SUBLANES, cw), F32), pltpu.VMEM(sshape, F32),
                        pltpu.VMEM((tt, width_out), F32)],
        compiler_params=_params(("parallel", "arbitrary")),
        name=name,
    )(x, z, sm, buf, state0, *vecs, *consts)


def _gdn(x, z, sm, buf, s0, conv_w, a_log, dt_bias, norm_g, batch, t_pad, tt, tv_last):
    ltri, utri = _chunk_masks(tt, min(CHUNK, tt))
    vecs = [conv_w, _lane_vec(a_log, SM_A), _lane_vec(dt_bias, SM_A), jnp.tile(norm_g, H_B).reshape(1, W_B)]
    consts = [_expand_matrix(SM_BETA, H_B, DK_B), _expand_matrix(SM_A, H_B, DK_B), ltri, utri,
              _block_diag(2 * H_B * DK_B, DK_B, 1.0), _block_diag(W_B, DV_B, 1.0 / DV_B)]
    return _recurrent_call(_gdn_kernel, "gdn", x, z, sm, buf, s0, vecs, consts, batch, t_pad, tt, tv_last, W_B)


def _ssd(x, z, sm, buf, h0, conv_w, conv_b, a_log, dt_bias, dskip, norm_g, batch, t_pad, tt, tv_last):
    ltri, utri = _chunk_masks(tt, min(CHUNK, tt))
    vecs = [conv_w, conv_b.reshape(1, CC), jnp.repeat(a_log, P_C).reshape(1, W_C), _lane_vec(dt_bias, SM_DT),
            jnp.repeat(dskip, P_C).reshape(1, W_C), norm_g.reshape(1, W_C)]
    consts = [_expand_matrix(SM_DT, H_C, P_C), ltri, utri, _block_diag(W_C, W_C // G_C, float(G_C) / W_C)]
    return _recurrent_call(_ssd_kernel, "ssd", x, z, sm, buf, h0, vecs, consts, batch, t_pad, tt, tv_last, W_C)


def _dsa_sample_select_kernel(pt_ref, qi_ref, w_ref, kin_ref, *rest, topk, n_q, idx_bits, pg, group):
    pages = rest[:pg]
    bias_ref, keys_ref = rest[pg:]
    j = pl.program_id(1)
    n_s = pl.num_programs(1)
    n_tiles = bias_ref.shape[1]
    shape = (SUBLANES, LANES)
    row = lax.broadcasted_iota(I32, shape, 0)
    lane = lax.broadcasted_iota(I32, shape, 1)
    imin = jnp.int32(INT_MIN)

    def keys_of(s, valid):
        r = jnp.maximum(s, 0.0) * w_ref[0]
        per_q = [jnp.sum(r[q * H_I:(q + 1) * H_I], axis=0, keepdims=True) for q in range(n_q)]
        sc = jnp.concatenate(per_q + [jnp.zeros((SUBLANES - n_q, LANES), F32)], axis=0)
        return jnp.where(valid, _sortable(sc), imin)

    qi = qi_ref[0]
    ss = [_dot(qi, pages[t][0, 0].astype(MM)) for t in range(pg)]
    for t in range(pg):
        keys_ref[j * pg + t] = keys_of(ss[t], row < n_q)

    @pl.when(j == n_s - 1)
    def _():
        keys_ref[n_tiles - 1] = keys_of(_dot(qi, kin_ref[0]), (row < n_q) & (lane <= row))
        _select_topk(keys_ref, bias_ref.at[0], n_tiles // group, group, topk, idx_bits, shape, 1)


def _dsa_sample_attend_kernel(pt_ref, q_ref, bias_ref, bnew_ref, kn_ref, vn_ref, *rest, pg):
    kps, vps = rest[:pg], rest[pg:2 * pg]
    o_ref, qbd_ref, m_ref, l_ref, acc_ref = rest[2 * pg:]
    j = pl.program_id(1)
    n_s = pl.num_programs(1)
    rows = H_A * SUBLANES
    shape = (rows, LANES)
    head_of_row = lax.broadcasted_iota(I32, (rows, W_A), 0) // SUBLANES
    head_of_lane = lax.broadcasted_iota(I32, (rows, W_A), 1) // DH_A
    own = head_of_row == head_of_lane

    @pl.when(j == 0)
    def _():
        qrep = jnp.concatenate([q_ref[0]] * H_A, axis=0)
        qbd_ref[...] = jnp.where(own, qrep, jnp.zeros_like(qrep))
        m_ref[...] = jnp.full(shape, NEG, F32)
        l_ref[...] = jnp.zeros(shape, F32)
        acc_ref[...] = jnp.zeros((rows, W_A), F32)

    def absorb(kks, vvs, biases):
        qbd = qbd_ref[...]
        ss = [_dot(qbd, kk) for kk in kks]
        bs = [jnp.concatenate([b] * H_A, axis=0) for b in biases]
        ss = [s + b for s, b in zip(ss, bs)]
        m_old = m_ref[...]
        mx = jnp.max(functools.reduce(jnp.maximum, ss), axis=1, keepdims=True)
        m_new = jnp.maximum(m_old, jnp.broadcast_to(mx, shape))
        alpha = jnp.exp2(m_old - m_new)
        ps = [jnp.where(b == 0.0, jnp.exp2(s - m_new), 0.0) for s, b in zip(ss, bs)]
        psum = jnp.sum(functools.reduce(jnp.add, ps), axis=1, keepdims=True)
        l_ref[...] = l_ref[...] * alpha + jnp.broadcast_to(psum, shape)
        pv = functools.reduce(jnp.add, [_dot_nt(p.astype(MM), vv) for p, vv in zip(ps, vvs)])
        acc_ref[...] = acc_ref[...] * jnp.concatenate([alpha] * (W_A // LANES), axis=1) + pv
        m_ref[...] = m_new

    absorb([kp[0, 0].astype(MM) for kp in kps], [vp[0, 0].astype(MM) for vp in vps],
           [bias_ref[0, t] for t in range(pg)])

    @pl.when(j == n_s - 1)
    def _():
        absorb([kn_ref[0]], [vn_ref[0]], [bnew_ref[0, 0]])
        o = acc_ref[...] / jnp.concatenate([l_ref[...]] * (W_A // LANES), axis=1)
        o = jnp.where(own, o, 0.0)
        out = functools.reduce(jnp.add, [o[h * SUBLANES:(h + 1) * SUBLANES] for h in range(H_A)])
        o_ref[0] = out.astype(o_ref.dtype)


def _dsa_sample(layer, q16, qi16, sm32, sm16, k16, v16, cache_k, cache_v, cache_ki, page_table, bs, ts):
    n_pages = page_table.shape[1]
    page = cache_ki.shape[2]
    assert page == LANES and ts <= SUBLANES
    past = n_pages * page
    topk = min(TOPK, (past + ts) // 4)
    idx_bits = int(past + ts).bit_length() + 1
    pt = page_table.reshape(-1)
    rq = ts * H_I
    qi_r = qi16.reshape(bs, rq, D_I)
    wi = sm32[:, SM_WI:SM_WI + H_I] * ((D_I ** -0.5) * (H_I ** -0.5))
    w_r = jnp.broadcast_to(wi.reshape(bs, rq, 1), (bs, rq, LANES))
    pad_rows = lambda a, rows: jnp.pad(a.reshape(bs, ts, -1), ((0, 0), (0, rows - ts), (0, 0)))
    new_t = lambda a: jnp.pad(jnp.swapaxes(a.reshape(bs, ts, -1), 1, 2), ((0, 0), (0, 0), (0, page - ts)))
    kin = new_t(sm16[:, SM_KI:SM_KI + D_I])
    per_b = lambda shp: pl.BlockSpec((1,) + shp, lambda b, j, p: (b,) + (0,) * len(shp))
    pg = math.gcd(n_pages, 16)
    n_steps = n_pages // pg
    paged = lambda w, t: pl.BlockSpec(
        (1, 1, w, page), lambda b, j, p: (layer, p[b * n_pages + j * pg + t], 0, 0))
    tiles = n_pages + 1
    group = max(g for g in range(1, 9) if tiles % g == 0)

    bias = pl.pallas_call(
        functools.partial(_dsa_sample_select_kernel, topk=topk, n_q=ts, idx_bits=idx_bits, pg=pg, group=group),
        grid_spec=pltpu.PrefetchScalarGridSpec(
            num_scalar_prefetch=1, grid=(bs, n_steps),
            in_specs=[per_b((rq, D_I)), per_b((rq, LANES)), per_b((D_I, page))]
                     + [paged(D_I, t) for t in range(pg)],
            out_specs=per_b((tiles, SUBLANES, LANES)),
            scratch_shapes=[pltpu.VMEM((tiles, SUBLANES, LANES), I32)]),
        out_shape=jax.ShapeDtypeStruct((bs, tiles, SUBLANES, LANES), F32),
        compiler_params=_params(("parallel", "arbitrary")),
        name="dsa_sample_select",
    )(pt, qi_r, w_r, kin, *([jnp.swapaxes(cache_ki, 2, 3)] * pg))

    page_t = lambda c: jnp.transpose(c, (0, 1, 3, 4, 2)).reshape(c.shape[:2] + (W_A, page))
    ck = page_t(cache_k)
    cv = page_t(cache_v)
    rows = H_A * SUBLANES
    out = pl.pallas_call(
        functools.partial(_dsa_sample_attend_kernel, pg=pg),
        grid_spec=pltpu.PrefetchScalarGridSpec(
            num_scalar_prefetch=1, grid=(bs, n_steps),
            in_specs=[per_b((SUBLANES, W_A)),
                      pl.BlockSpec((1, pg, SUBLANES, LANES), lambda b, j, p: (b, j, 0, 0)),
                      pl.BlockSpec((1, 1, SUBLANES, LANES), lambda b, j, p: (b, n_pages, 0, 0)),
                      per_b((W_A, page)), per_b((W_A, page))]
                     + [paged(W_A, t) for t in range(pg)] * 2,
            out_specs=per_b((SUBLANES, W_A)),
            scratch_shapes=[pltpu.VMEM((rows, W_A), MM), pltpu.VMEM((rows, LANES), F32),
                            pltpu.VMEM((rows, LANES), F32), pltpu.VMEM((rows, W_A), F32)]),
        out_shape=jax.ShapeDtypeStruct((bs, SUBLANES, W_A), MM),
        compiler_params=_params(("parallel", "arbitrary")),
        name="dsa_sample_attend",
    )(pt, pad_rows(q16, SUBLANES), bias, bias, new_t(k16), new_t(v16), *([ck] * pg), *([cv] * pg))
    return out[:, :ts].reshape(bs * ts, W_A)


def _pad_seq(a, batch, t, t_pad):
    if t == t_pad:
        return a
    return jnp.pad(a.reshape(batch, t, -1), ((0, 0), (0, t_pad - t), (0, 0))).reshape(batch * t_pad, -1)


def _trunk(x, p, lw, conv_b0, ssm_b0, conv_c0, ssm_c0, kv, tm, tt):
    batch, t, d = x.shape
    n = batch * t
    depth = len(lw)
    t_pad = -(-t // tt) * tt
    tv_last = t - (t_pad - tt)
    assert min(CHUNK, tt) == DK_B == P_C and tv_last >= CONV_W - 1
    h = x.reshape(n, d)
    rec = [[] for _ in range(7)]
    for l, w in enumerate(lw):
        if kv is None:
            assert t % tm == 0
            (q16, k32, k16, v32, v16t, qi16, sm32, sm16, bqkv, bz, cz, cxbc, smt) = _inproj(
                h, w["g_mix"], w["w_in"], w["q_norm"], w["k_norm"], tm, feature_major_seq=t)
            ya = _dsa_prompt(q16, qi16, smt, k16, v16t, sm16, batch, t, tm)
            heads_last = lambda a: jnp.transpose(a.reshape(batch, H_A, DH_A, t), (0, 3, 1, 2))
            k_rows, v_rows = heads_last(k32), heads_last(v32)
        else:
            (q16, k32, k16, v32, v16, qi16, sm32, sm16, bqkv, bz, cz, cxbc) = _inproj(
                h, w["g_mix"], w["w_in"], w["q_norm"], w["k_norm"], tm)
            ya = _dsa_sample(l, q16, qi16, sm32, sm16, k16, v16, kv[0], kv[1], kv[2], kv[3], batch, t)
            k_rows, v_rows = k32.reshape(batch, t, H_A, DH_A), v32.reshape(batch, t, H_A, DH_A)
        pad = lambda a: _pad_seq(a, batch, t, t_pad)
        unpad = lambda a: a if t == t_pad else a.reshape(batch, t_pad, -1)[:, :t].reshape(n, -1)
        smp = pad(sm32)
        yb, bconv, bs_ = _gdn(pad(bqkv), pad(bz), smp, conv_b0[l], ssm_b0[l], w["b_conv_w"], w["b_A_log"],
                              w["b_dt_bias"], w["b_norm"], batch, t_pad, tt, tv_last)
        yc, cconv, cs_ = _ssd(pad(cxbc), pad(cz), smp, conv_c0[l], ssm_c0[l], w["c_conv_w"], w["c_conv_b"],
                              w["c_A_log"], w["c_dt_bias"], w["c_D"], w["c_norm"], batch, t_pad, tt, tv_last)
        h = _channel_mix(h, ya, unpad(yb), unpad(yc), p[l].reshape(n, -1), w["w_out"], w["g_ffn"], w["g_ple"],
                         w["w_ple_gate"], w["w_ple_proj"], tm, FF_TILE, ffn=w.get("ffn"), moe=w.get("moe"))
        for r, a in zip(rec, (k_rows, v_rows,
                              sm32[:, SM_KI:SM_KI + D_I].reshape(batch, t, D_I), bconv, bs_, cconv, cs_)):
            r.append(a)
    return h.reshape(batch, t, d), [jnp.stack(r) for r in rec]


def kernel(x_prompt, x_sample, cache_k, cache_v, cache_kidx, state_b_conv, state_b_ssm, state_c_conv, state_c_ssm, page_table, p_prompt, p_sample, w_in, w_out, g_mix, g_ffn, g_ple, q_norm, k_norm, b_conv_w, b_A_log, b_dt_bias, b_norm, c_conv_w, c_conv_b, c_A_log, c_dt_bias, c_D, c_norm, ffn_w1, ffn_w3, ffn_w2, moe_router, moe_w1, moe_w3, moe_w2, w_ple_gate, w_ple_proj):
    depth = w_in.shape[0]
    lw = []
    for l in range(depth):
        w = dict(w_in=_arrange_w_in(w_in[l]), w_out=w_out[l].astype(MM), g_mix=g_mix[l], g_ffn=g_ffn[l],
                 g_ple=g_ple[l], q_norm=q_norm[l], k_norm=k_norm[l], b_conv_w=b_conv_w[l], b_A_log=b_A_log[l],
                 b_dt_bias=b_dt_bias[l], b_norm=b_norm[l], c_conv_w=c_conv_w[l], c_conv_b=c_conv_b[l],
                 c_A_log=c_A_log[l], c_dt_bias=c_dt_bias[l], c_D=c_D[l], c_norm=c_norm[l],
                 w_ple_gate=w_ple_gate[l].astype(MM), w_ple_proj=w_ple_proj[l].astype(MM))
        j = l // 2
        if l % 2 == 0:
            w["ffn"] = (ffn_w1[j].astype(MM), ffn_w3[j].astype(MM), ffn_w2[j].astype(MM))
        else:
            w["moe"] = (moe_router[j], moe_w1[j].astype(MM), moe_w3[j].astype(MM), moe_w2[j].astype(MM))
        lw.append(w)

    bp, tp, _ = x_prompt.shape
    bs, ts, _ = x_sample.shape
    zeros = lambda *s: jnp.zeros((depth, bp) + s, F32)
    y_p, (k_p, v_p, ki_p, bc_p, bs_p, cc_p, cs_p) = _trunk(
        x_prompt, p_prompt, lw, zeros(CONV_W - 1, CB), zeros(H_B, DK_B, DV_B), zeros(CONV_W - 1, CC),
        zeros(H_C, P_C, N_C), None, min(512, bp * tp), min(256, tp))
    y_s, (k_s, v_s, ki_s, bc_s, bs_s, cc_s, cs_s) = _trunk(
        x_sample, p_sample, lw, state_b_conv, state_b_ssm, state_c_conv, state_c_ssm,
        (cache_k, cache_v, cache_kidx, page_table), bs * ts, CHUNK)
    return (y_p, y_s, k_p, k_s, v_p, v_s, ki_p, ki_s, bc_p, bc_s, bs_p, bs_s, cc_p, cc_s, cs_p, cs_s)
```

```python
import functools
import math

import numpy as np
import jax
import jax.numpy as jnp
from jax import lax
from jax.experimental import pallas as pl
from jax.experimental.pallas import tpu as pltpu

F32 = jnp.float32
I32 = jnp.int32
MM = jnp.bfloat16
EPS = 1e-6
NEG = -1e30
INT_MIN = -2 ** 31
LOG2E = 1.4426950408889634

LANES = 128
SUBLANES = 8
VMEM_LIMIT = 56 * 1024 * 1024

H_A, DH_A, H_I, D_I, TOPK, Q_BLOCK = 8, 64, 8, 64, 256, 128
H_B, DK_B, DV_B = 4, 64, 64
H_C, P_C, G_C, N_C = 4, 64, 2, 128
CONV_W, CHUNK = 4, 64
N_EXPERTS, TOP_E = 8, 2
FF_TILE = 1792
W_A, W_B, W_C = H_A * DH_A, H_B * DV_B, H_C * P_C
CB = 2 * H_B * DK_B + H_B * DV_B
CC = W_C + 2 * G_C * N_C

SM_KI, SM_WI, SM_BETA, SM_A, SM_DT = 0, 64, 72, 76, 80


def _dot(a, b):
    return jnp.dot(a, b, preferred_element_type=F32)


def _dot_nt(a, b):
    return lax.dot_general(a, b, (((1,), (1,)), ((), ())), preferred_element_type=F32)


def _dot_tn(a, b):
    return lax.dot_general(a, b, (((0,), (0,)), ((), ())), preferred_element_type=F32)


def _split2(x):
    hi = x.astype(MM)
    lo = (x - hi.astype(F32)).astype(MM)
    return hi, lo


def _split3(x):
    a1 = x.astype(MM)
    r = x - a1.astype(F32)
    a2 = r.astype(MM)
    a3 = (r - a2.astype(F32)).astype(MM)
    return a1, a2, a3


def _dot_hi(a, b, f=_dot):
    ah, al = _split2(a)
    bh, bl = _split2(b)
    return f(ah, bh) + (f(ah, bl) + f(al, bh))


def _dot_x3(a, b_exact):
    a1, a2, a3 = _split3(a)
    return _dot(a1, b_exact) + (_dot(a2, b_exact) + _dot(a3, b_exact))


def _x3_dot(a_exact, b):
    b1, b2, b3 = _split3(b)
    return _dot(a_exact, b1) + (_dot(a_exact, b2) + _dot(a_exact, b3))


def _group_mean(xx, bd):
    hi, lo = _split2(xx)
    return _dot(hi, bd) + _dot(lo, bd)


def _sigmoid(x):
    return 1.0 / (1.0 + jnp.exp(-x))


def _silu(x):
    return x * _sigmoid(x)


def _softplus(x):
    return jnp.maximum(x, 0.0) + jnp.log(1.0 + jnp.exp(-jnp.abs(x)))


def _rms(x, g):
    ms = jnp.mean(x * x, axis=-1, keepdims=True)
    return x * lax.rsqrt(ms + EPS) * g


def _block_diag(width, group, value):
    i = np.arange(width)
    return jnp.asarray(np.where((i[:, None] // group) == (i[None, :] // group), value, 0.0), MM)


def _params(sem):
    return pltpu.CompilerParams(dimension_semantics=sem, vmem_limit_bytes=VMEM_LIMIT)


def _full(shape):
    n = len(shape)
    return pl.BlockSpec(shape, lambda *_: (0,) * n, pipeline_mode=pl.Buffered(1))


def _inproj_kernel(h_ref, g_ref, w_ref, qn_ref, kn_ref, bd_ref,
                   q16_ref, k32_ref, k16_ref, v32_ref, v16_ref, qi16_ref,
                   sm32_ref, sm16_ref, bqkv_ref, bz_ref, cz_ref, cxbc_ref, *smt_ref, feature_major):
    u = _rms(h_ref[...], g_ref[...]).astype(MM)

    def proj(a, b):
        return _dot_nt(u, w_ref[a:b, :])

    bd = bd_ref[...]
    aq = proj(0, W_A)
    q = aq * lax.rsqrt(_group_mean(aq * aq, bd) + EPS) * qn_ref[...]
    q16_ref[...] = (q * (DH_A ** -0.5 * LOG2E)).astype(MM)
    ak = proj(W_A, 2 * W_A)
    k = ak * lax.rsqrt(_group_mean(ak * ak, bd) + EPS) * kn_ref[...]
    k16_ref[...] = k.astype(MM)
    v = proj(2 * W_A, 3 * W_A)
    qi16_ref[...] = proj(3 * W_A, 4 * W_A).astype(MM)
    o = 4 * W_A
    sm = proj(o, o + LANES)
    sm32_ref[...] = sm
    sm16_ref[...] = sm.astype(MM)
    if feature_major:
        vt = v.T
        k32_ref[0] = k.T
        v32_ref[0] = vt
        v16_ref[0] = vt.astype(MM)
        smt_ref[0][...] = sm.T
    else:
        k32_ref[...] = k
        v32_ref[...] = v
        v16_ref[...] = v.astype(MM)
    o += LANES
    bqkv_ref[...] = proj(o, o + CB)
    bz_ref[...] = proj(o + CB, o + CB + W_B)
    o += CB + W_B
    cz_ref[...] = proj(o, o + W_C)
    cxbc_ref[...] = proj(o + W_C, o + W_C + CC)


def _arrange_w_in(w):
    w = w.T
    o_aik = 4 * W_A
    o_aiw = o_aik + D_I
    o_bqkv = o_aiw + H_I
    o_bz = o_bqkv + CB
    o_bbeta = o_bz + W_B
    o_ba = o_bbeta + H_B
    o_cz = o_ba + H_B
    o_cxbc = o_cz + W_C
    o_cdt = o_cxbc + CC
    small = jnp.concatenate(
        [w[o_aik:o_aiw], w[o_aiw:o_bqkv], w[o_bbeta:o_ba], w[o_ba:o_cz], w[o_cdt:o_cdt + H_C]], axis=0)
    small = jnp.pad(small, ((0, LANES - small.shape[0]), (0, 0)))
    return jnp.concatenate([w[:o_aik], small, w[o_bqkv:o_bbeta], w[o_cz:o_cdt]], axis=0).astype(MM)


def _inproj(h, g, w_arr, qn, kn, tm, feature_major_seq=None):
    n, d = h.shape
    wtot = w_arr.shape[0]
    bd = _block_diag(W_A, DH_A, 1.0 / DH_A)
    row = lambda w: pl.BlockSpec((tm, w), lambda i: (i, 0))
    rowout = lambda w, t: (row(w), jax.ShapeDtypeStruct((n, w), t))
    outs = [rowout(W_A, MM), rowout(W_A, F32), rowout(W_A, MM), rowout(W_A, F32), rowout(W_A, MM),
            rowout(W_A, MM), rowout(LANES, F32), rowout(LANES, MM), rowout(CB, F32), rowout(W_B, F32),
            rowout(W_C, F32), rowout(CC, F32)]
    if feature_major_seq is not None:
        tpb = feature_major_seq // tm
        per_seq = (pl.BlockSpec((1, W_A, tm), lambda i: (i // tpb, 0, i % tpb)),
                   jax.ShapeDtypeStruct((n // feature_major_seq, W_A, feature_major_seq), F32))
        outs[1] = per_seq
        outs[3] = per_seq
        outs[4] = (pl.BlockSpec((1, W_A, tm), lambda i: (i, 0, 0)),
                   jax.ShapeDtypeStruct((n // tm, W_A, tm), MM))
        outs.append((pl.BlockSpec((LANES, tm), lambda i: (0, i)), jax.ShapeDtypeStruct((LANES, n), F32)))
    return pl.pallas_call(
        functools.partial(_inproj_kernel, feature_major=feature_major_seq is not None),
        grid=(n // tm,),
        in_specs=[row(d), _full((1, d)), _full((wtot, d)), _full((1, W_A)), _full((1, W_A)),
                  _full((W_A, W_A))],
        out_specs=[s for s, _ in outs],
        out_shape=[t for _, t in outs],
        compiler_params=_params(("parallel",)),
        name="inproj",
    )(h, g.reshape(1, d), w_arr, jnp.tile(qn, H_A).reshape(1, W_A), jnp.tile(kn, H_A).reshape(1, W_A), bd)


def _mix_prologue(h_ref, ya_ref, yb_ref, yc_ref, wo_ref, gf_ref, h1_ref, u2_ref, acc_ref):
    h1 = h_ref[...] + (_dot(ya_ref[...], wo_ref[0:W_A, :])
                       + _dot(yb_ref[...], wo_ref[W_A:W_A + W_B, :])
                       + _dot(yc_ref[...], wo_ref[W_A + W_B:, :]))
    h1_ref[...] = h1
    u2 = _rms(h1, gf_ref[...])
    u2_ref[...] = u2.astype(MM)
    acc_ref[...] = jnp.zeros_like(acc_ref)
    return u2


def _mix_epilogue(p_ref, gp_ref, wg_ref, wp_ref, h1_ref, acc_ref, o_ref):
    h2 = h1_ref[...] + acc_ref[...]
    u3 = _rms(h2, gp_ref[...]).astype(MM)
    gate = _sigmoid(_dot(u3, wg_ref[...]))
    o_ref[...] = h2 + gate * _dot(p_ref[...].astype(MM), wp_ref[...])


def _ffn_kernel(h_ref, ya_ref, yb_ref, yc_ref, p_ref, wo_ref, gf_ref, w1_ref, w3_ref, w2_ref,
                gp_ref, wg_ref, wp_ref, o_ref, h1_ref, u2_ref, acc_ref):
    f = pl.program_id(1)

    @pl.when(f == 0)
    def _():
        _mix_prologue(h_ref, ya_ref, yb_ref, yc_ref, wo_ref, gf_ref, h1_ref, u2_ref, acc_ref)

    u2 = u2_ref[...]
    hid = _silu(_dot(u2, w1_ref[...])) * _dot(u2, w3_ref[...])
    acc_ref[...] += _dot(hid.astype(MM), w2_ref[...])

    @pl.when(f == pl.num_programs(1) - 1)
    def _():
        _mix_epilogue(p_ref, gp_ref, wg_ref, wp_ref, h1_ref, acc_ref, o_ref)


def _moe_kernel(h_ref, ya_ref, yb_ref, yc_ref, p_ref, wo_ref, gf_ref, wr_ref, w1_ref, w3_ref, w2_ref,
                gp_ref, wg_ref, wp_ref, o_ref, h1_ref, u2_ref, acc_ref, comb_ref):
    e = pl.program_id(1)
    f = pl.program_id(2)

    @pl.when((e == 0) & (f == 0))
    def _():
        u2 = _mix_prologue(h_ref, ya_ref, yb_ref, yc_ref, wo_ref, gf_ref, h1_ref, u2_ref, acc_ref)
        logits = _dot_hi(u2, wr_ref[...])
        lane = lax.broadcasted_iota(I32, logits.shape, 1)
        big = jnp.int32(LANES)
        lg = jnp.where(lane < N_EXPERTS, logits, -jnp.inf)
        m1 = jnp.max(lg, axis=1, keepdims=True)
        i1 = jnp.min(jnp.where(lg == m1, lane, big), axis=1, keepdims=True)
        lg2 = jnp.where(lane == i1, -jnp.inf, lg)
        m2 = jnp.max(lg2, axis=1, keepdims=True)
        i2 = jnp.min(jnp.where(lg2 == m2, lane, big), axis=1, keepdims=True)
        e2 = jnp.exp(m2 - m1)
        g1 = 1.0 / (1.0 + e2)
        g2 = e2 / (1.0 + e2)
        for ex in range(N_EXPERTS):
            ce = jnp.where(i1 == ex, g1, 0.0) + jnp.where(i2 == ex, g2, 0.0)
            comb_ref[ex] = jnp.broadcast_to(ce, comb_ref.shape[1:])

    u2 = u2_ref[...]
    hid = _silu(_dot(u2, w1_ref[0])) * _dot(u2, w3_ref[0])
    tf = hid.shape[1]
    ce = comb_ref[e]
    hid = hid * jnp.concatenate([ce] * (tf // LANES), axis=1)
    acc_ref[...] += _dot(hid.astype(MM), w2_ref[0])

    @pl.when((e == pl.num_programs(1) - 1) & (f == pl.num_programs(2) - 1))
    def _():
        _mix_epilogue(p_ref, gp_ref, wg_ref, wp_ref, h1_ref, acc_ref, o_ref)


def _channel_mix(h, ya, yb, yc, p, wo, gf, gp, wg, wp, tm, tf, *, ffn=None, moe=None):
    n, d = h.shape
    dple = p.shape[1]
    if ffn is not None:
        w1, w3, w2 = ffn
        dff = w1.shape[1]
        grid = (n // tm, dff // tf)
        row = lambda w: pl.BlockSpec((tm, w), lambda i, f: (i, 0))
        wspecs = [pl.BlockSpec((d, tf), lambda i, f: (0, f)), pl.BlockSpec((d, tf), lambda i, f: (0, f)),
                  pl.BlockSpec((tf, d), lambda i, f: (f, 0))]
        wargs = [w1, w3, w2]
        body = _ffn_kernel
        extra_scratch = []
        sem = ("parallel", "arbitrary")
    else:
        wr, w1, w3, w2 = moe
        ne, _, dff = w1.shape
        grid = (n // tm, ne, dff // tf)
        row = lambda w: pl.BlockSpec((tm, w), lambda i, e, f: (i, 0))
        wspecs = [_full((d, LANES)),
                  pl.BlockSpec((1, d, tf), lambda i, e, f: (e, 0, f)),
                  pl.BlockSpec((1, d, tf), lambda i, e, f: (e, 0, f)),
                  pl.BlockSpec((1, tf, d), lambda i, e, f: (e, f, 0))]
        wargs = [jnp.pad(wr, ((0, 0), (0, LANES - ne))), w1, w3, w2]
        body = _moe_kernel
        extra_scratch = [pltpu.VMEM((ne, tm, LANES), F32)]
        sem = ("parallel", "arbitrary", "arbitrary")
    return pl.pallas_call(
        body,
        grid=grid,
        in_specs=[row(d), row(W_A), row(W_B), row(W_C), row(dple), _full((d, d)), _full((1, d))]
                 + wspecs + [_full((1, d)), _full((d, d)), _full((dple, d))],
        out_specs=row(d),
        out_shape=jax.ShapeDtypeStruct((n, d), F32),
        scratch_shapes=[pltpu.VMEM((tm, d), F32), pltpu.VMEM((tm, d), MM), pltpu.VMEM((tm, d), F32)]
                       + extra_scratch,
        compiler_params=_params(sem),
        name="ffn" if ffn is not None else "moe",
    )(h, ya, yb, yc, p, wo, gf.reshape(1, d), *wargs, gp.reshape(1, d), wg, wp)


def _sortable(x):
    b = lax.bitcast_convert_type(x, I32)
    return b ^ ((b >> 31) & jnp.int32(0x7FFFFFFF))


def _select_topk(keys_ref, bias_ref, n_groups, group, topk, idx_bits, tile_shape, key_axis):
    imin = jnp.int32(INT_MIN)
    pos = lax.broadcasted_iota(I32, tile_shape, key_axis)
    if key_axis == 1:
        qshape = tile_shape
        full = lambda x: x
        acc0 = jnp.zeros(tile_shape, F32)
        add_tile = lambda acc, m: acc + m
        finish = lambda acc: jnp.broadcast_to(jnp.sum(acc, axis=1, keepdims=True), tile_shape)
    else:
        qshape = (1, tile_shape[1])
        full = lambda x: jnp.broadcast_to(x, tile_shape)
        acc0 = jnp.zeros((SUBLANES, tile_shape[1]), F32)

        def add_tile(acc, m):
            parts = [m[s * SUBLANES:(s + 1) * SUBLANES] for s in range(tile_shape[0] // SUBLANES)]
            while len(parts) > 1:
                parts = [a + b for a, b in zip(parts[::2], parts[1::2])]
            return acc + parts[0]

        finish = lambda acc: jnp.sum(acc, axis=0, keepdims=True)

    def over_groups(body, init):
        if isinstance(n_groups, int) and n_groups * group <= 128:
            for i in range(n_groups):
                init = body(i, init)
            return init
        return lax.fori_loop(0, n_groups, body, init)

    def count(pred):
        def body(i, acc):
            for t in range(group):
                j = i * group + t
                acc = add_tile(acc, jnp.where(pred(keys_ref[j], j), 1.0, 0.0))
            return acc
        return finish(over_groups(body, acc0))

    def value_bit(b, carry):
        thr, cnt_thr = carry
        cand = thr + jnp.left_shift(jnp.int32(1), 31 - b)
        cand_f = full(cand)
        cnt = count(lambda k, j: k >= cand_f)
        keep = cnt >= topk
        return jnp.where(keep, cand, thr), jnp.where(keep, cnt, cnt_thr)

    thr, cnt_thr = lax.fori_loop(0, 32, value_bit, (jnp.full(qshape, imin), jnp.full(qshape, 2.0 ** 30, F32)))
    no_cut = jnp.full(qshape, 2 ** idx_bits - 1, I32)
    tied = (cnt_thr > topk) & (thr > imin)
    floor = jnp.maximum(thr, imin + 1)
    floor_f = full(floor)

    def index_cut():
        need = topk - count(lambda k, j: k > floor_f)

        def index_bit(b, c):
            cc = c | jnp.left_shift(jnp.int32(1), idx_bits - 1 - b)
            cc_f = full(cc)
            cnt = count(lambda k, j: (k == floor_f) & (j * LANES + pos < cc_f))
            return jnp.where(cnt <= need, cc, c)

        return lax.fori_loop(0, idx_bits, index_bit, jnp.zeros(qshape, I32))

    cut = lax.cond(jnp.max(jnp.where(tied, 1.0, 0.0)) > 0.0, index_cut, lambda: no_cut)
    cut_f = full(jnp.where(tied, cut, no_cut))

    def write(i, carry):
        for t in range(group):
            j = i * group + t
            k = keys_ref[j]
            sel = (k > floor_f) | ((k == floor_f) & (j * LANES + pos < cut_f))
            bias_ref[j] = jnp.where(sel, 0.0, NEG)
        return carry

    over_groups(write, 0)


def _two_buffer_pipeline(n, products, finish, back, bufs):
    a, b = bufs
    finish(0, products(0), a)

    def step(c, cur, nxt):
        @pl.when(c + 1 < n)
        def _():
            back(c, cur, lambda: products(c + 1), lambda vals: finish(c + 1, vals, nxt))

        @pl.when(c + 1 == n)
        def _():
            back(c, cur, lambda: None, lambda vals: None)

    def body(k, carry):
        step(2 * k, a, b)
        step(2 * k + 1, b, a)
        return carry

    lax.fori_loop(0, (n + 1) // 2, body, 0)


def _dsa_prompt_kernel(q_ref, qi_ref, smt_ref, k_ref, vt_ref, ki_ref, o_ref,
                       keys_ref, bias_ref, qp_ref, qip_ref, m_ref, mb_ref, l_ref, acc_ref,
                       sta_ref, stb_ref, *, topk, kc, idx_bits):
    i = pl.program_id(1)
    qb = q_ref.shape[0]
    tpc = kc // LANES
    n_chunks = ((i + 1) * qb + kc - 1) // kc
    n_pairs = H_A // 2
    tile = (LANES, qb)
    key_pos = lax.broadcasted_iota(I32, tile, 0)
    q_pos = i * qb + lax.broadcasted_iota(I32, tile, 1)
    low = lax.broadcasted_iota(I32, (qb, LANES), 1) < DH_A

    for h in range(H_A):
        pr, half = h // 2, h % 2
        rows = slice(half * qb, (half + 1) * qb)
        blk = q_ref[:, pr * LANES:(pr + 1) * LANES]
        qp_ref[pr, rows, :] = jnp.where(low if half == 0 else ~low, blk, jnp.zeros_like(blk))
        blk = qi_ref[:, pr * LANES:(pr + 1) * LANES]
        if half == 1:
            blk = pltpu.roll(blk, DH_A, 1)
        qip_ref[pr, rows, :] = jnp.where(low, blk, jnp.zeros_like(blk))
    wt = smt_ref[SM_WI:SM_WI + H_I, :] * ((D_I ** -0.5) * (H_I ** -0.5))

    def score_chunk(c, carry):
        base = pl.multiple_of(c * kc, kc)
        kt = ki_ref[pl.ds(base, kc), :]
        ss = [_dot_nt(kt, qip_ref[pr]) for pr in range(n_pairs)]
        for t in range(tpc):
            sc = jnp.zeros(tile, F32)
            for h in range(H_I):
                blk = ss[h // 2][t * LANES:(t + 1) * LANES, (h % 2) * qb:(h % 2 + 1) * qb]
                sc = sc + wt[h:h + 1, :] * jnp.maximum(blk, 0.0)
            valid = base + t * LANES + key_pos <= q_pos
            keys_ref[c * tpc + t] = jnp.where(valid, _sortable(sc), jnp.int32(INT_MIN))
        return carry

    lax.fori_loop(0, n_chunks, score_chunk, 0)
    _select_topk(keys_ref, bias_ref, n_chunks, tpc, topk, idx_bits, tile, 0)

    m_ref[...] = jnp.full(m_ref.shape, NEG, F32)
    mb_ref[...] = jnp.full(mb_ref.shape, NEG, F32)
    l_ref[...] = jnp.zeros(l_ref.shape, F32)
    acc_ref[...] = jnp.zeros(acc_ref.shape, F32)

    def products(c):
        base = pl.multiple_of(c * kc, kc)
        return [_dot_nt(k_ref[pl.ds(base, kc), pr * LANES:(pr + 1) * LANES], qp_ref[pr])
                for pr in range(n_pairs)]

    def front(c, prods, buf):
        for pr in range(n_pairs):
            st = []
            for t in range(tpc):
                b = bias_ref[c * tpc + t]
                x = prods[pr][t * LANES:(t + 1) * LANES, :] + jnp.concatenate([b, b], axis=1)
                buf[pr, t * LANES:(t + 1) * LANES, :] = x
                st.append(x)
            cmax = jnp.max(functools.reduce(jnp.maximum, st), axis=0, keepdims=True)
            m_ref[pr] = jnp.maximum(m_ref[pr], cmax)

    def back(c, buf, start_next, finish_next):
        m_now = [m_ref[pr] for pr in range(n_pairs)]
        nxt = start_next()
        for pr in range(n_pairs):
            alpha = jnp.exp2(mb_ref[pr] - m_now[pr])
            ps = [jnp.exp2(buf[pr, t * LANES:(t + 1) * LANES, :] - m_now[pr]) for t in range(tpc)]
            l_ref[pr] = l_ref[pr] * alpha + jnp.sum(functools.reduce(jnp.add, ps), axis=0, keepdims=True)
            p = jnp.concatenate(ps, axis=0).astype(MM)
            acc_ref[pr] = acc_ref[pr] * alpha + _dot(vt_ref[c, pr * LANES:(pr + 1) * LANES, :], p)
            mb_ref[pr] = m_now[pr]
        finish_next(nxt)

    _two_buffer_pipeline(n_chunks, products, front, back, (sta_ref, stb_ref))
    for pr in range(n_pairs):
        o = acc_ref[pr] / l_ref[pr]
        ot = jnp.concatenate([o[:DH_A, :qb], o[DH_A:, qb:]], axis=0)
        o_ref[:, pr * LANES:(pr + 1) * LANES] = ot.T.astype(o_ref.dtype)


def _dsa_prompt(q16, qi16, smt, k16, vt, sm16, batch, seq, kc):
    qb = Q_BLOCK
    topk = min(TOPK, seq // 4)
    nq = seq // qb
    idx_bits = int(seq).bit_length() + 1
    assert H_A == H_I and DH_A == D_I and qb == LANES
    n_kc = seq // kc
    n_pairs = H_A // 2
    blk = lambda w: pl.BlockSpec((qb, w), lambda b, i: (b * nq + i, 0))
    seqblk = lambda w: pl.BlockSpec((seq, w), lambda b, i: (b, 0), pipeline_mode=pl.Buffered(1))
    tiles = (seq // LANES, LANES, qb)
    return pl.pallas_call(
        functools.partial(_dsa_prompt_kernel, topk=topk, kc=kc, idx_bits=idx_bits),
        grid=(batch, nq),
        in_specs=[blk(W_A), blk(W_A), pl.BlockSpec((LANES, qb), lambda b, i: (0, b * nq + i)), seqblk(W_A),
                  pl.BlockSpec((n_kc, W_A, kc), lambda b, i: (b, 0, 0), pipeline_mode=pl.Buffered(1)),
                  seqblk(LANES)],
        out_specs=blk(W_A),
        out_shape=jax.ShapeDtypeStruct((batch * seq, W_A), MM),
        scratch_shapes=[pltpu.VMEM(tiles, I32), pltpu.VMEM(tiles, F32),
                        pltpu.VMEM((n_pairs, 2 * qb, LANES), MM), pltpu.VMEM((n_pairs, 2 * qb, LANES), MM),
                        pltpu.VMEM((n_pairs, 1, 2 * qb), F32), pltpu.VMEM((n_pairs, 1, 2 * qb), F32),
                        pltpu.VMEM((n_pairs, 1, 2 * qb), F32), pltpu.VMEM((n_pairs, LANES, 2 * qb), F32),
                        pltpu.VMEM((n_pairs, kc, 2 * qb), F32), pltpu.VMEM((n_pairs, kc, 2 * qb), F32)],
        compiler_params=_params(("parallel", "arbitrary")),
        name="dsa_prompt",
    )(q16, qi16, smt, k16, vt, sm16)


def _expand_matrix(first_lane, heads, width):
    m = np.zeros((LANES, heads * width), np.float32)
    for h in range(heads):
        m[first_lane + h, h * width:(h + 1) * width] = 1.0
    return jnp.asarray(m, MM)


def _chunk_masks(tt, chunk):
    i = np.arange(tt)
    same = (i[:, None] // chunk) == (i[None, :] // chunk)
    ltri = jnp.asarray(np.where(same & (i[:, None] >= i[None, :]), 1.0, 0.0), MM)
    utri = jnp.asarray(np.where(same & (i[:, None] <= i[None, :]), 1.0, 0.0), F32)
    return ltri, utri


def _lane_vec(values, first_lane):
    return jnp.zeros((1, LANES), F32).at[0, first_lane:first_lane + values.shape[0]].set(values)


def _conv_tile(x_ref, buf_ref, cw_ref, xe_ref, nb_ref, t, n_t, tv_last):
    tt = x_ref.shape[0]
    hist = SUBLANES

    @pl.when(t == 0)
    def _():
        xe_ref[0:hist, :] = jnp.zeros((hist, xe_ref.shape[1]), F32)
        xe_ref[hist - (CONV_W - 1):hist, :] = buf_ref[0]

    xe_ref[hist:hist + tt, :] = x_ref[...]
    y = jnp.zeros(x_ref.shape, F32)
    for i in range(CONV_W):
        o = hist - (CONV_W - 1) + i
        y = y + xe_ref[o:o + tt, :] * cw_ref[i:i + 1, :]

    @pl.when(t == n_t - 1)
    def _():
        nb_ref[0] = xe_ref[hist + tv_last - (CONV_W - 1):hist + tv_last, :]

    xe_ref[0:hist, :] = xe_ref[tt:tt + hist, :]
    return y


def _live_rows(shape, t, n_t, tv_last):
    row = lax.broadcasted_iota(I32, shape, 0)
    return (t < n_t - 1) | (row < tv_last)


def _row_cumsum(vals, utri, ones_c, heads, width, chunk):
    tt = vals.shape[0]
    out = []
    for h in range(heads):
        col = vals[:, h * width:h * width + chunk]
        out.append(_x3_dot(ones_c, jnp.concatenate([col] * (tt // chunk), axis=1) * utri))
    return out


def _gdn_kernel(x_ref, z_ref, sm_ref, buf_ref, s0_ref, cw_ref, alog_ref, dtb_ref, ng_ref,
                ebeta_ref, eg_ref, ltri_ref, utri_ref, bd1_ref, bdn_ref,
                y_ref, nb_ref, st_ref, xe_ref, s_ref, o_ref, *, tv_last, chunk):
    t = pl.program_id(1)
    n_t = pl.num_programs(1)
    tt = x_ref.shape[0]
    wq = H_B * DK_B

    @pl.when(t == 0)
    def _():
        s_ref[...] = s0_ref[0]

    qkv = _silu(_conv_tile(x_ref, buf_ref, cw_ref, xe_ref, nb_ref, t, n_t, tv_last))
    qk = qkv[:, :2 * wq]
    qk = qk * lax.rsqrt(_group_mean(qk * qk, bd1_ref[...]) + EPS)
    q = qk[:, :wq] * (DK_B ** -0.5)
    k = qk[:, wq:]
    v = qkv[:, 2 * wq:]

    sm = sm_ref[...]
    beta = _sigmoid(sm)
    g = -jnp.exp(alog_ref[...]) * _softplus(sm + dtb_ref[...])
    if tv_last < tt:
        live = _live_rows(sm.shape, t, n_t, tv_last)
        beta = jnp.where(live, beta, 0.0)
        g = jnp.where(live, g, 0.0)
    betab = _dot_x3(beta, ebeta_ref[...])
    gb = _dot_x3(g, eg_ref[...])
    gc = _x3_dot(ltri_ref[...], gb)
    ones_c = jnp.ones((chunk, tt), MM)
    gr = _row_cumsum(gb, utri_ref[...], ones_c, H_B, DK_B, chunk)
    eg = jnp.exp(gc)
    kb = k * betab
    vb = v * betab
    kbeg = kb * eg
    qg = q * eg
    ii = lax.broadcasted_iota(I32, (chunk, chunk), 0)
    jj = lax.broadcasted_iota(I32, (chunk, chunk), 1)
    levels = max(1, (chunk - 1).bit_length())

    n_c = tt // chunk
    units = [(c, h) for c in range(n_c) for h in range(H_B)]
    rows = lambda c: slice(c * chunk, (c + 1) * chunk)
    cols = lambda h: slice(h * DK_B, (h + 1) * DK_B)
    k16 = k.astype(MM)
    kk = {u: _dot_nt(kb[rows(u[0]), cols(u[1])].astype(MM), k16[rows(u[0]), cols(u[1])]) for u in units}
    qk = {u: _dot_nt(q[rows(u[0]), cols(u[1])].astype(MM), k16[rows(u[0]), cols(u[1])]) for u in units}
    xs, ps, attn = {}, {}, {}
    for u in units:
        c, h = u
        decay = jnp.exp(jnp.where(ii >= jj, gc[rows(c), cols(h)] - gr[h][:, rows(c)], NEG))
        ps[u] = -jnp.where(ii > jj, kk[u] * decay, 0.0)
        attn[u] = (qk[u] * decay).astype(MM)
        xs[u] = jnp.concatenate([vb[rows(c), cols(h)], kbeg[rows(c), cols(h)]], axis=1)
    wx = 2 * DV_B
    for lv in range(levels):
        last_lv = lv == levels - 1
        prod = {}
        for u in units:
            ph, pl_ = _split2(ps[u])
            xh, xl = _split2(xs[u])
            bh = xh if last_lv else jnp.concatenate([xh, ph], axis=1)
            bl = xl if last_lv else jnp.concatenate([xl, pl_], axis=1)
            prod[u] = _dot(ph, bh) + (_dot(ph, bl) + _dot(pl_, bh))
        for u in units:
            xs[u] = xs[u] + prod[u][:, :wx]
            if not last_lv:
                ps[u] = prod[u][:, wx:]

    for c in range(n_c):
        last = gc[(c + 1) * chunk - 1:(c + 1) * chunk, :]
        kg = (k[rows(c), :] * jnp.exp(last - gc[rows(c), :])).astype(MM)
        gl = jnp.exp(last)
        qg16 = qg[rows(c), :].astype(MM)
        s_old = [s_ref[h] for h in range(H_B)]
        s16 = [s.astype(MM) for s in s_old]
        ws = [_dot(xs[c, h][:, DV_B:].astype(MM), s16[h]) for h in range(H_B)]
        qs = [_dot(qg16[:, cols(h)], s16[h]) for h in range(H_B)]
        u16 = [(xs[c, h][:, :DV_B] - ws[h]).astype(MM) for h in range(H_B)]
        au = [_dot(attn[c, h], u16[h]) for h in range(H_B)]
        ku = [_dot_tn(kg[:, cols(h)], u16[h]) for h in range(H_B)]
        for h in range(H_B):
            o_ref[rows(c), cols(h)] = qs[h] + au[h]
            s_ref[h] = s_old[h] * gl[:, h * DK_B:h * DK_B + 1] + ku[h]

    o = o_ref[...]
    o = o * lax.rsqrt(_group_mean(o * o, bdn_ref[...]) + EPS) * ng_ref[...]
    y_ref[...] = (o * _silu(z_ref[...])).astype(y_ref.dtype)

    @pl.when(t == n_t - 1)
    def _():
        st_ref[0] = s_ref[...]


def _ssd_kernel(x_ref, z_ref, sm_ref, buf_ref, h0_ref, cw_ref, cb_ref, alog_ref, dtb_ref, d_ref, ng_ref,
                edt_ref, ltri_ref, utri_ref, bdn_ref,
                y_ref, nb_ref, ht_ref, xe_ref, h_ref, o_ref, *, tv_last, chunk):
    t = pl.program_id(1)
    n_t = pl.num_programs(1)
    tt = x_ref.shape[0]
    gw = G_C * N_C
    hpg = H_C // G_C

    @pl.when(t == 0)
    def _():
        h_ref[...] = h0_ref[0]

    xbc = _silu(_conv_tile(x_ref, buf_ref, cw_ref, xe_ref, nb_ref, t, n_t, tv_last) + cb_ref[...])
    xs = xbc[:, :W_C]
    bm = xbc[:, W_C:W_C + gw].astype(MM)
    cm = xbc[:, W_C + gw:].astype(MM)

    dt = _softplus(sm_ref[...] + dtb_ref[...])
    if tv_last < tt:
        dt = jnp.where(_live_rows(dt.shape, t, n_t, tv_last), dt, 0.0)
    dtb = _dot_x3(dt, edt_ref[...])
    a = dtb * (-jnp.exp(alog_ref[...]))
    xdt = xs * dtb
    ac = _x3_dot(ltri_ref[...], a)
    ones_c = jnp.ones((chunk, tt), MM)
    ar = _row_cumsum(a, utri_ref[...], ones_c, H_C, P_C, chunk)
    ea = jnp.exp(ac)
    ii = lax.broadcasted_iota(I32, (chunk, chunk), 0)
    jj = lax.broadcasted_iota(I32, (chunk, chunk), 1)

    for c in range(tt // chunk):
        rs = slice(c * chunk, (c + 1) * chunk)
        last = ac[(c + 1) * chunk - 1:(c + 1) * chunk, :]
        xdec = (xdt[rs, :] * jnp.exp(last - ac[rs, :])).astype(MM)
        hdec = jnp.exp(last)
        for g in range(G_C):
            gs = slice(g * N_C, (g + 1) * N_C)
            cbm = _dot_nt(cm[rs, gs], bm[rs, gs])
            for h in range(g * hpg, (g + 1) * hpg):
                hs = slice(h * P_C, (h + 1) * P_C)
                lmat = jnp.exp(jnp.where(ii >= jj, ac[rs, hs] - ar[h][:, rs], NEG))
                hprev = h_ref[h]
                y = _dot((cbm * lmat).astype(MM), xdt[rs, hs].astype(MM))
                y = y + ea[rs, hs] * _dot_nt(cm[rs, gs], hprev.astype(MM))
                o_ref[rs, hs] = y + d_ref[:, hs] * xs[rs, hs]
                h_ref[h] = hprev * hdec[:, h * P_C:h * P_C + 1] + _dot_tn(xdec[:, hs], bm[rs, gs])

    yz = o_ref[...] * _silu(z_ref[...])
    y_ref[...] = (yz * lax.rsqrt(_group_mean(yz * yz, bdn_ref[...]) + EPS) * ng_ref[...]).astype(y_ref.dtype)

    @pl.when(t == n_t - 1)
    def _():
        ht_ref[0] = h_ref[...]


def _recurrent_call(body, name, x, z, sm, buf, state0, vecs, consts, batch, t_pad, tt, tv_last, width_out):
    n_t = t_pad // tt
    cw = x.shape[1]
    blk = lambda w: pl.BlockSpec((tt, w), lambda b, t: (b * n_t + t, 0))
    per_b = lambda shp: pl.BlockSpec((1,) + shp, lambda b, t: (b,) + (0,) * len(shp))
    sshape = state0.shape[1:]
    return pl.pallas_call(
        functools.partial(body, tv_last=tv_last, chunk=min(CHUNK, tt)),
        grid=(batch, n_t),
        in_specs=[blk(cw), blk(width_out), blk(LANES), per_b((CONV_W - 1, cw)), per_b(sshape)]
                 + [_full(v.shape) for v in vecs] + [_full(c.shape) for c in consts],
        out_specs=[blk(width_out), per_b((CONV_W - 1, cw)), per_b(sshape)],
        out_shape=[jax.ShapeDtypeStruct((batch * t_pad, width_out), MM),
                   jax.ShapeDtypeStruct((batch, CONV_W - 1, cw), F32),
                   jax.ShapeDtypeStruct((batch,) + sshape, F32)],
        scratch_shapes=[pltpu.VMEM((tt + SUBLANES, cw), F32), pltpu.VMEM(sshape, F32),
                        pltpu.VMEM((tt, width_out), F32)],
        compiler_params=_params(("parallel", "arbitrary")),
        name=name,
    )(x, z, sm, buf, state0, *vecs, *consts)


def _gdn(x, z, sm, buf, s0, conv_w, a_log, dt_bias, norm_g, batch, t_pad, tt, tv_last):
    ltri, utri = _chunk_masks(tt, min(CHUNK, tt))
    vecs = [conv_w, _lane_vec(a_log, SM_A), _lane_vec(dt_bias, SM_A), jnp.tile(norm_g, H_B).reshape(1, W_B)]
    consts = [_expand_matrix(SM_BETA, H_B, DK_B), _expand_matrix(SM_A, H_B, DK_B), ltri, utri,
              _block_diag(2 * H_B * DK_B, DK_B, 1.0), _block_diag(W_B, DV_B, 1.0 / DV_B)]
    return _recurrent_call(_gdn_kernel, "gdn", x, z, sm, buf, s0, vecs, consts, batch, t_pad, tt, tv_last, W_B)


def _ssd(x, z, sm, buf, h0, conv_w, conv_b, a_log, dt_bias, dskip, norm_g, batch, t_pad, tt, tv_last):
    ltri, utri = _chunk_masks(tt, min(CHUNK, tt))
    vecs = [conv_w, conv_b.reshape(1, CC), jnp.repeat(a_log, P_C).reshape(1, W_C), _lane_vec(dt_bias, SM_DT),
            jnp.repeat(dskip, P_C).reshape(1, W_C), norm_g.reshape(1, W_C)]
    consts = [_expand_matrix(SM_DT, H_C, P_C), ltri, utri, _block_diag(W_C, W_C // G_C, float(G_C) / W_C)]
    return _recurrent_call(_ssd_kernel, "ssd", x, z, sm, buf, h0, vecs, consts, batch, t_pad, tt, tv_last, W_C)


def _dsa_sample_select_kernel(pt_ref, qi_ref, w_ref, kin_ref, *rest, topk, n_q, idx_bits, pg, group):
    pages = rest[:pg]
    bias_ref, keys_ref = rest[pg:]
    j = pl.program_id(1)
    n_s = pl.num_programs(1)
    n_tiles = bias_ref.shape[1]
    shape = (SUBLANES, LANES)
    row = lax.broadcasted_iota(I32, shape, 0)
    lane = lax.broadcasted_iota(I32, shape, 1)
    imin = jnp.int32(INT_MIN)

    def keys_of(s, valid):
        r = jnp.maximum(s, 0.0) * w_ref[0]
        per_q = [jnp.sum(r[q * H_I:(q + 1) * H_I], axis=0, keepdims=True) for q in range(n_q)]
        sc = jnp.concatenate(per_q + [jnp.zeros((SUBLANES - n_q, LANES), F32)], axis=0)
        return jnp.where(valid, _sortable(sc), imin)

    qi = qi_ref[0]
    ss = [_dot(qi, pages[t][0, 0].astype(MM)) for t in range(pg)]
    for t in range(pg):
        keys_ref[j * pg + t] = keys_of(ss[t], row < n_q)

    @pl.when(j == n_s - 1)
    def _():
        keys_ref[n_tiles - 1] = keys_of(_dot(qi, kin_ref[0]), (row < n_q) & (lane <= row))
        _select_topk(keys_ref, bias_ref.at[0], n_tiles // group, group, topk, idx_bits, shape, 1)


def _dsa_sample_attend_kernel(pt_ref, q_ref, bias_ref, bnew_ref, kn_ref, vn_ref, *rest, pg):
    kps, vps = rest[:pg], rest[pg:2 * pg]
    o_ref, qbd_ref, m_ref, l_ref, acc_ref = rest[2 * pg:]
    j = pl.program_id(1)
    n_s = pl.num_programs(1)
    rows = H_A * SUBLANES
    shape = (rows, LANES)
    head_of_row = lax.broadcasted_iota(I32, (rows, W_A), 0) // SUBLANES
    head_of_lane = lax.broadcasted_iota(I32, (rows, W_A), 1) // DH_A
    own = head_of_row == head_of_lane

    @pl.when(j == 0)
    def _():
        qrep = jnp.concatenate([q_ref[0]] * H_A, axis=0)
        qbd_ref[...] = jnp.where(own, qrep, jnp.zeros_like(qrep))
        m_ref[...] = jnp.full(shape, NEG, F32)
        l_ref[...] = jnp.zeros(shape, F32)
        acc_ref[...] = jnp.zeros((rows, W_A), F32)

    def absorb(kks, vvs, biases):
        qbd = qbd_ref[...]
        ss = [_dot(qbd, kk) for kk in kks]
        bs = [jnp.concatenate([b] * H_A, axis=0) for b in biases]
        ss = [s + b for s, b in zip(ss, bs)]
        m_old = m_ref[...]
        mx = jnp.max(functools.reduce(jnp.maximum, ss), axis=1, keepdims=True)
        m_new = jnp.maximum(m_old, jnp.broadcast_to(mx, shape))
        alpha = jnp.exp2(m_old - m_new)
        ps = [jnp.where(b == 0.0, jnp.exp2(s - m_new), 0.0) for s, b in zip(ss, bs)]
        psum = jnp.sum(functools.reduce(jnp.add, ps), axis=1, keepdims=True)
        l_ref[...] = l_ref[...] * alpha + jnp.broadcast_to(psum, shape)
        pv = functools.reduce(jnp.add, [_dot_nt(p.astype(MM), vv) for p, vv in zip(ps, vvs)])
        acc_ref[...] = acc_ref[...] * jnp.concatenate([alpha] * (W_A // LANES), axis=1) + pv
        m_ref[...] = m_new

    absorb([kp[0, 0].astype(MM) for kp in kps], [vp[0, 0].astype(MM) for vp in vps],
           [bias_ref[0, t] for t in range(pg)])

    @pl.when(j == n_s - 1)
    def _():
        absorb([kn_ref[0]], [vn_ref[0]], [bnew_ref[0, 0]])
        o = acc_ref[...] / jnp.concatenate([l_ref[...]] * (W_A // LANES), axis=1)
        o = jnp.where(own, o, 0.0)
        out = functools.reduce(jnp.add, [o[h * SUBLANES:(h + 1) * SUBLANES] for h in range(H_A)])
        o_ref[0] = out.astype(o_ref.dtype)


def _dsa_sample(layer, q16, qi16, sm32, sm16, k16, v16, cache_k, cache_v, cache_ki, page_table, bs, ts):
    n_pages = page_table.shape[1]
    page = cache_ki.shape[2]
    assert page == LANES and ts <= SUBLANES
    past = n_pages * page
    topk = min(TOPK, (past + ts) // 4)
    idx_bits = int(past + ts).bit_length() + 1
    pt = page_table.reshape(-1)
    rq = ts * H_I
    qi_r = qi16.reshape(bs, rq, D_I)
    wi = sm32[:, SM_WI:SM_WI + H_I] * ((D_I ** -0.5) * (H_I ** -0.5))
    w_r = jnp.broadcast_to(wi.reshape(bs, rq, 1), (bs, rq, LANES))
    pad_rows = lambda a, rows: jnp.pad(a.reshape(bs, ts, -1), ((0, 0), (0, rows - ts), (0, 0)))
    new_t = lambda a: jnp.pad(jnp.swapaxes(a.reshape(bs, ts, -1), 1, 2), ((0, 0), (0, 0), (0, page - ts)))
    kin = new_t(sm16[:, SM_KI:SM_KI + D_I])
    per_b = lambda shp: pl.BlockSpec((1,) + shp, lambda b, j, p: (b,) + (0,) * len(shp))
    pgs = math.gcd(n_pages, 64)
    pg = math.gcd(n_pages, 16)
    n_steps = n_pages // pg
    paged = lambda w, t, per: pl.BlockSpec(
        (1, 1, w, page), lambda b, j, p: (layer, p[b * n_pages + j * per + t], 0, 0))
    tiles = n_pages + 1
    group = max(g for g in range(1, 9) if tiles % g == 0)

    bias = pl.pallas_call(
        functools.partial(_dsa_sample_select_kernel, topk=topk, n_q=ts, idx_bits=idx_bits, pg=pgs, group=group),
        grid_spec=pltpu.PrefetchScalarGridSpec(
            num_scalar_prefetch=1, grid=(bs, n_pages // pgs),
            in_specs=[per_b((rq, D_I)), per_b((rq, LANES)), per_b((D_I, page))]
                     + [paged(D_I, t, pgs) for t in range(pgs)],
            out_specs=per_b((tiles, SUBLANES, LANES)),
            scratch_shapes=[pltpu.VMEM((tiles, SUBLANES, LANES), I32)]),
        out_shape=jax.ShapeDtypeStruct((bs, tiles, SUBLANES, LANES), F32),
        compiler_params=_params(("parallel", "arbitrary")),
        name="dsa_sample_select",
    )(pt, qi_r, w_r, kin, *([jnp.swapaxes(cache_ki, 2, 3)] * pgs))

    page_t = lambda c: jnp.transpose(c, (0, 1, 3, 4, 2)).reshape(c.shape[:2] + (W_A, page))
    ck = page_t(cache_k)
    cv = page_t(cache_v)
    rows = H_A * SUBLANES
    out = pl.pallas_call(
        functools.partial(_dsa_sample_attend_kernel, pg=pg),
        grid_spec=pltpu.PrefetchScalarGridSpec(
            num_scalar_prefetch=1, grid=(bs, n_steps),
            in_specs=[per_b((SUBLANES, W_A)),
                      pl.BlockSpec((1, pg, SUBLANES, LANES), lambda b, j, p: (b, j, 0, 0)),
                      pl.BlockSpec((1, 1, SUBLANES, LANES), lambda b, j, p: (b, n_pages, 0, 0)),
                      per_b((W_A, page)), per_b((W_A, page))]
                     + [paged(W_A, t, pg) for t in range(pg)] * 2,
            out_specs=per_b((SUBLANES, W_A)),
            scratch_shapes=[pltpu.VMEM((rows, W_A), MM), pltpu.VMEM((rows, LANES), F32),
                            pltpu.VMEM((rows, LANES), F32), pltpu.VMEM((rows, W_A), F32)]),
        out_shape=jax.ShapeDtypeStruct((bs, SUBLANES, W_A), MM),
        compiler_params=_params(("parallel", "arbitrary")),
        name="dsa_sample_attend",
    )(pt, pad_rows(q16, SUBLANES), bias, bias, new_t(k16), new_t(v16), *([ck] * pg), *([cv] * pg))
    return out[:, :ts].reshape(bs * ts, W_A)


def _pad_seq(a, batch, t, t_pad):
    if t == t_pad:
        return a
    return jnp.pad(a.reshape(batch, t, -1), ((0, 0), (0, t_pad - t), (0, 0))).reshape(batch * t_pad, -1)


def _trunk(x, p, lw, conv_b0, ssm_b0, conv_c0, ssm_c0, kv, tm, tt):
    batch, t, d = x.shape
    n = batch * t
    depth = len(lw)
    t_pad = -(-t // tt) * tt
    tv_last = t - (t_pad - tt)
    assert min(CHUNK, tt) == DK_B == P_C and tv_last >= CONV_W - 1
    h = x.reshape(n, d)
    rec = [[] for _ in range(7)]
    for l, w in enumerate(lw):
        if kv is None:
            assert t % tm == 0
            (q16, k32, k16, v32, v16t, qi16, sm32, sm16, bqkv, bz, cz, cxbc, smt) = _inproj(
                h, w["g_mix"], w["w_in"], w["q_norm"], w["k_norm"], tm, feature_major_seq=t)
            ya = _dsa_prompt(q16, qi16, smt, k16, v16t, sm16, batch, t, tm)
            heads_last = lambda a: jnp.transpose(a.reshape(batch, H_A, DH_A, t), (0, 3, 1, 2))
            k_rows, v_rows = heads_last(k32), heads_last(v32)
        else:
            (q16, k32, k16, v32, v16, qi16, sm32, sm16, bqkv, bz, cz, cxbc) = _inproj(
                h, w["g_mix"], w["w_in"], w["q_norm"], w["k_norm"], tm)
            ya = _dsa_sample(l, q16, qi16, sm32, sm16, k16, v16, kv[0], kv[1], kv[2], kv[3], batch, t)
            k_rows, v_rows = k32.reshape(batch, t, H_A, DH_A), v32.reshape(batch, t, H_A, DH_A)
        pad = lambda a: _pad_seq(a, batch, t, t_pad)
        unpad = lambda a: a if t == t_pad else a.reshape(batch, t_pad, -1)[:, :t].reshape(n, -1)
        smp = pad(sm32)
        yb, bconv, bs_ = _gdn(pad(bqkv), pad(bz), smp, conv_b0[l], ssm_b0[l], w["b_conv_w"], w["b_A_log"],
                              w["b_dt_bias"], w["b_norm"], batch, t_pad, tt, tv_last)
        yc, cconv, cs_ = _ssd(pad(cxbc), pad(cz), smp, conv_c0[l], ssm_c0[l], w["c_conv_w"], w["c_conv_b"],
                              w["c_A_log"], w["c_dt_bias"], w["c_D"], w["c_norm"], batch, t_pad, tt, tv_last)
        h = _channel_mix(h, ya, unpad(yb), unpad(yc), p[l].reshape(n, -1), w["w_out"], w["g_ffn"], w["g_ple"],
                         w["w_ple_gate"], w["w_ple_proj"], tm, FF_TILE, ffn=w.get("ffn"), moe=w.get("moe"))
        for r, a in zip(rec, (k_rows, v_rows,
                              sm32[:, SM_KI:SM_KI + D_I].reshape(batch, t, D_I), bconv, bs_, cconv, cs_)):
            r.append(a)
    return h.reshape(batch, t, d), [jnp.stack(r) for r in rec]


def kernel(x_prompt, x_sample, cache_k, cache_v, cache_kidx, state_b_conv, state_b_ssm, state_c_conv, state_c_ssm, page_table, p_prompt, p_sample, w_in, w_out, g_mix, g_ffn, g_ple, q_norm, k_norm, b_conv_w, b_A_log, b_dt_bias, b_norm, c_conv_w, c_conv_b, c_A_log, c_dt_bias, c_D, c_norm, ffn_w1, ffn_w3, ffn_w2, moe_router, moe_w1, moe_w3, moe_w2, w_ple_gate, w_ple_proj):
    depth = w_in.shape[0]
    lw = []
    for l in range(depth):
        w = dict(w_in=_arrange_w_in(w_in[l]), w_out=w_out[l].astype(MM), g_mix=g_mix[l], g_ffn=g_ffn[l],
                 g_ple=g_ple[l], q_norm=q_norm[l], k_norm=k_norm[l], b_conv_w=b_conv_w[l], b_A_log=b_A_log[l],
                 b_dt_bias=b_dt_bias[l], b_norm=b_norm[l], c_conv_w=c_conv_w[l], c_conv_b=c_conv_b[l],
                 c_A_log=c_A_log[l], c_dt_bias=c_dt_bias[l], c_D=c_D[l], c_norm=c_norm[l],
                 w_ple_gate=w_ple_gate[l].astype(MM), w_ple_proj=w_ple_proj[l].astype(MM))
        j = l // 2
        if l % 2 == 0:
            w["ffn"] = (ffn_w1[j].astype(MM), ffn_w3[j].astype(MM), ffn_w2[j].astype(MM))
        else:
            w["moe"] = (moe_router[j], moe_w1[j].astype(MM), moe_w3[j].astype(MM), moe_w2[j].astype(MM))
        lw.append(w)

    bp, tp, _ = x_prompt.shape
    bs, ts, _ = x_sample.shape
    zeros = lambda *s: jnp.zeros((depth, bp) + s, F32)
    y_p, (k_p, v_p, ki_p, bc_p, bs_p, cc_p, cs_p) = _trunk(
        x_prompt, p_prompt, lw, zeros(CONV_W - 1, CB), zeros(H_B, DK_B, DV_B), zeros(CONV_W - 1, CC),
        zeros(H_C, P_C, N_C), None, min(512, bp * tp), min(256, tp))
    y_s, (k_s, v_s, ki_s, bc_s, bs_s, cc_s, cs_s) = _trunk(
        x_sample, p_sample, lw, state_b_conv, state_b_ssm, state_c_conv, state_c_ssm,
        (cache_k, cache_v, cache_kidx, page_table), bs * ts, CHUNK)
    return (y_p, y_s, k_p, k_s, v_p, v_s, ki_p, ki_s, bc_p, bc_s, bs_p, bs_s, cc_p, cc_s, cs_p, cs_s)
```
